```python
import math
import jax, jax.numpy as jnp
from jax import lax
import numpy as np

D_MODEL = 1024
BATCH = 1
SEQ = 16384
DEPTH = 4

MIX_WIDTH = D_MODEL
ATTN_WIDTH = MIX_WIDTH // 2
POOL_WIDTH = MIX_WIDTH - ATTN_WIDTH
DIFF_HEAD_DIM = 64
N_DIFF_HEADS = ATTN_WIDTH // (2 * DIFF_HEAD_DIM)
N_POOL_GROUPS = 4
POOL_GROUP_WIDTH = POOL_WIDTH // N_POOL_GROUPS
POOL_WINDOWS = (2, 4, 8, 16)
D_FF = ((8 * D_MODEL // 3 + 255) // 256) * 256
IN_WIDTH = 3 * ATTN_WIDTH + POOL_WIDTH
ROPE_THETA = 10000.0
Q_BLOCK = 128
NORM_EPS = 1e-6

kernel_name = "hymba_diffattn_pool_macaron"


def rms_norm(x, g):
    xf = x.astype(jnp.float32)
    y = xf * lax.rsqrt(jnp.mean(xf * xf, axis=-1, keepdims=True) + NORM_EPS)
    return (y * g.astype(jnp.float32)).astype(x.dtype)


def swiglu(h, w_gate, w_up, w_down):
    return (jax.nn.silu(h @ w_gate) * (h @ w_up)) @ w_down


def rope_tables(seq, dim):
    inv_freq = ROPE_THETA ** (-jnp.arange(0, dim, 2, dtype=jnp.float32) / dim)
    ang = jnp.arange(seq, dtype=jnp.float32)[:, None] * inv_freq[None, :]
    ang = jnp.concatenate([ang, ang], axis=-1)
    return jnp.cos(ang), jnp.sin(ang)


def apply_rope(t, cos, sin):
    half = t.shape[-1] // 2
    t1, t2 = t[..., :half], t[..., half:]
    rot = jnp.concatenate([-t2, t1], axis=-1)
    c = cos[None, :, None, None, :]
    s = sin[None, :, None, None, :]
    return (t.astype(jnp.float32) * c + rot.astype(jnp.float32) * s).astype(t.dtype)


def diff_attention(q, k, v, lam):
    B, S, H, _, d = q.shape
    nb = S // Q_BLOCK
    scale = d ** -0.5
    kt = jnp.transpose(k, (0, 2, 3, 1, 4))
    vt = jnp.transpose(v, (0, 2, 1, 3))
    qb = (q * scale).reshape(B, nb, Q_BLOCK, H, 2, d).transpose(1, 0, 3, 4, 2, 5)
    key_pos = jnp.arange(S, dtype=jnp.int32)
    starts = jnp.arange(nb, dtype=jnp.int32) * Q_BLOCK

    def block(args):
        q_blk, start = args
        s = jnp.einsum('bhcqd,bhckd->bhcqk', q_blk, kt).astype(jnp.float32)
        q_pos = start + jnp.arange(Q_BLOCK, dtype=jnp.int32)
        mask = key_pos[None, :] <= q_pos[:, None]
        s = jnp.where(mask, s, -jnp.inf)
        p = jax.nn.softmax(s, axis=-1)
        a = p[:, :, 0] - lam * p[:, :, 1]
        return jnp.einsum('bhqk,bhke->bhqe', a.astype(v.dtype), vt)

    out = lax.map(block, (qb, starts))
    return out.transpose(1, 0, 3, 2, 4).reshape(B, S, H, 2 * d)


def pool_mixer(u, w, scale):
    B, S, _ = u.shape
    ug = u.reshape(B, S, N_POOL_GROUPS, POOL_GROUP_WIDTH).astype(jnp.float32)
    cs = jnp.cumsum(ug, axis=1)
    cs = jnp.concatenate([jnp.zeros_like(cs[:, :1]), cs], axis=1)
    pos = jnp.arange(S, dtype=jnp.int32)[:, None]
    win = jnp.array(POOL_WINDOWS, dtype=jnp.int32)[None, :]
    lo = jnp.maximum(pos + 1 - win, 0)
    grp = jnp.arange(N_POOL_GROUPS, dtype=jnp.int32)[None, :]
    window_sum = cs[:, 1:] - cs[:, lo, grp]
    count = jnp.minimum(pos + 1, win).astype(jnp.float32)
    diff = (window_sum / count[None, :, :, None] - ug).astype(u.dtype)
    y = jnp.einsum('bsgc,gce->bsge', diff, w).reshape(B, S, POOL_WIDTH)
    return y * scale


def setup_inputs(seed: int = 0) -> dict:
    key = jax.random.key(seed)
    ks = jax.random.split(key, 24)
    f32 = jnp.float32

    def normal(k, shape, s):
        return jax.random.normal(k, shape, dtype=f32) * s

    def gain(k, shape):
        return 1.0 + normal(k, shape, 0.02)

    return {
        "x": normal(ks[0], (BATCH, SEQ, D_MODEL), 1.0),
        "ffn1_norm": gain(ks[1], (DEPTH, D_MODEL)),
        "ffn1_w_gate": normal(ks[2], (DEPTH, D_MODEL, D_FF), D_MODEL ** -0.5),
        "ffn1_w_up": normal(ks[3], (DEPTH, D_MODEL, D_FF), D_MODEL ** -0.5),
        "ffn1_w_down": normal(ks[4], (DEPTH, D_FF, D_MODEL), D_FF ** -0.5),
        "mix_norm": gain(ks[5], (DEPTH, D_MODEL)),
        "w_in": normal(ks[6], (DEPTH, D_MODEL, IN_WIDTH), D_MODEL ** -0.5),
        "lambda_q1": normal(ks[7], (DEPTH, DIFF_HEAD_DIM), 0.1),
        "lambda_k1": normal(ks[8], (DEPTH, DIFF_HEAD_DIM), 0.1),
        "lambda_q2": normal(ks[9], (DEPTH, DIFF_HEAD_DIM), 0.1),
        "lambda_k2": normal(ks[10], (DEPTH, DIFF_HEAD_DIM), 0.1),
        "subln_gain": gain(ks[11], (DEPTH, 2 * DIFF_HEAD_DIM)),
        "pool_w": normal(ks[12], (DEPTH, N_POOL_GROUPS, POOL_GROUP_WIDTH, POOL_GROUP_WIDTH), POOL_GROUP_WIDTH ** -0.5),
        "pool_scale": gain(ks[13], (DEPTH, POOL_WIDTH)),
        "w_out": normal(ks[14], (DEPTH, MIX_WIDTH, D_MODEL), MIX_WIDTH ** -0.5),
        "ffn2_norm": gain(ks[15], (DEPTH, D_MODEL)),
        "ffn2_w_gate": normal(ks[16], (DEPTH, D_MODEL, D_FF), D_MODEL ** -0.5),
        "ffn2_w_up": normal(ks[17], (DEPTH, D_MODEL, D_FF), D_MODEL ** -0.5),
        "ffn2_w_down": normal(ks[18], (DEPTH, D_FF, D_MODEL), D_FF ** -0.5),
        "final_norm": gain(ks[19], (D_MODEL,)),
    }


def reference(x, ffn1_norm, ffn1_w_gate, ffn1_w_up, ffn1_w_down, mix_norm, w_in,
              lambda_q1, lambda_k1, lambda_q2, lambda_k2, subln_gain, pool_w, pool_scale,
              w_out, ffn2_norm, ffn2_w_gate, ffn2_w_up, ffn2_w_down, final_norm):
    B, S, _ = x.shape
    H, d = N_DIFF_HEADS, DIFF_HEAD_DIM
    cos, sin = rope_tables(S, d)

    for l in range(DEPTH):
        h = rms_norm(x, ffn1_norm[l])
        x = x + 0.5 * swiglu(h, ffn1_w_gate[l], ffn1_w_up[l], ffn1_w_down[l])

        h = rms_norm(x, mix_norm[l])
        proj = h @ w_in[l]
        q = proj[..., :ATTN_WIDTH].reshape(B, S, H, 2, d)
        k = proj[..., ATTN_WIDTH:2 * ATTN_WIDTH].reshape(B, S, H, 2, d)
        v = proj[..., 2 * ATTN_WIDTH:3 * ATTN_WIDTH].reshape(B, S, H, 2 * d)
        u = proj[..., 3 * ATTN_WIDTH:]

        q = apply_rope(q, cos, sin)
        k = apply_rope(k, cos, sin)
        lam_init = 0.8 - 0.6 * math.exp(-0.3 * l)
        lam = (jnp.exp(jnp.sum(lambda_q1[l].astype(jnp.float32) * lambda_k1[l].astype(jnp.float32)))
               - jnp.exp(jnp.sum(lambda_q2[l].astype(jnp.float32) * lambda_k2[l].astype(jnp.float32)))
               + lam_init)
        o = diff_attention(q, k, v, lam)
        o = rms_norm(o, subln_gain[l]) * (1.0 - lam_init)
        o = o.reshape(B, S, ATTN_WIDTH)

        p = pool_mixer(u, pool_w[l], pool_scale[l])

        x = x + jnp.concatenate([o, p], axis=-1) @ w_out[l]

        h = rms_norm(x, ffn2_norm[l])
        x = x + 0.5 * swiglu(h, ffn2_w_gate[l], ffn2_w_up[l], ffn2_w_down[l])

    return rms_norm(x, final_norm)
```

```python
import functools
import math

import jax
import jax.numpy as jnp
from jax import lax
from jax.experimental import pallas as pl
from jax.experimental.pallas import tpu as pltpu

F32 = jnp.float32
BF16 = jnp.bfloat16

HEAD_DIM = 64
HEAD_WIDTH = 2 * HEAD_DIM
POOL_WINDOWS = (2, 4, 8, 16)
POOL_HALO = 16
ROPE_THETA = 10000.0
NORM_EPS = 1e-6
NEG_INIT = -1e30

VMEM_LIMIT_BYTES = 56 * 1024 * 1024
ROW_TILE = 512
ATTN_TILE = 512


def _rms_norm(x, g):
    ms = jnp.mean(x * x, axis=-1, keepdims=True)
    return x * lax.rsqrt(ms + NORM_EPS) * g


def _compiler_params(n_axes):
    return pltpu.CompilerParams(
        dimension_semantics=("arbitrary",) * n_axes,
        vmem_limit_bytes=VMEM_LIMIT_BYTES,
    )


def _resident(shape):
    return pl.BlockSpec(shape, lambda *_: (0,) * len(shape), pipeline_mode=pl.Buffered(1))


def _ffn_kernel(x_ref, g_ref, wg_ref, wu_ref, wd_ref, o_ref):
    x = x_ref[...]
    h = _rms_norm(x, g_ref[...]).astype(BF16)
    gate = jnp.dot(h, wg_ref[...], preferred_element_type=F32)
    up = jnp.dot(h, wu_ref[...], preferred_element_type=F32)
    act = (gate * jax.nn.sigmoid(gate) * up).astype(BF16)
    y = jnp.dot(act, wd_ref[...], preferred_element_type=F32)
    o_ref[...] = x + 0.5 * y


def _ffn(x, g, wg, wu, wd):
    s, d = x.shape
    f = wg.shape[1]
    tm = ROW_TILE
    row = pl.BlockSpec((tm, d), lambda i: (i, 0))
    return pl.pallas_call(
        _ffn_kernel,
        grid=(s // tm,),
        in_specs=[row, _resident((1, d)), _resident((d, f)), _resident((d, f)), _resident((f, d))],
        out_specs=row,
        out_shape=jax.ShapeDtypeStruct((s, d), F32),
        compiler_params=_compiler_params(1),
        name="ffn",
    )(x, g, wg, wu, wd)


def _rope(t, cos, sin_signed, first_half):
    rot = jnp.where(first_half, pltpu.roll(t, HEAD_WIDTH - HEAD_DIM // 2, 1),
                    pltpu.roll(t, HEAD_DIM // 2, 1))
    return t * cos + rot * sin_signed


def _mix_in_kernel(x_ref, g_ref, w_ref, cos_ref, sin_ref, qz_ref, k_ref, vt_ref, u_ref, *, n_heads):
    attn_w = n_heads * HEAD_WIDTH
    h = _rms_norm(x_ref[...], g_ref[...]).astype(BF16)
    proj = jnp.dot(h, w_ref[...], preferred_element_type=F32)
    cos = cos_ref[...]
    sin_signed = sin_ref[...]
    tm = cos.shape[0]
    lane = lax.broadcasted_iota(jnp.int32, (tm, HEAD_WIDTH), 1)
    first_half = (lane % HEAD_DIM) < HEAD_DIM // 2
    map_row = lax.broadcasted_iota(jnp.int32, (HEAD_WIDTH, tm), 0)
    scale = HEAD_DIM ** -0.5
    for hh in range(n_heads):
        c0 = hh * HEAD_WIDTH
        q = _rope(proj[:, c0:c0 + HEAD_WIDTH], cos, sin_signed, first_half) * scale
        qt = q.T
        qz_ref[hh, 0] = jnp.where(map_row < HEAD_DIM, qt, 0.0).astype(BF16)
        qz_ref[hh, 1] = jnp.where(map_row >= HEAD_DIM, qt, 0.0).astype(BF16)
        k = _rope(proj[:, attn_w + c0:attn_w + c0 + HEAD_WIDTH], cos, sin_signed, first_half)
        k_ref[:, c0:c0 + HEAD_WIDTH] = k.astype(BF16)
        v = proj[:, 2 * attn_w + c0:2 * attn_w + c0 + HEAD_WIDTH]
        vt_ref[hh] = v.T.astype(BF16)
    u_ref[...] = proj[:, 3 * attn_w:]


def _mix_in(x, g, w_in, cos, sin_signed, n_heads):
    s, d = x.shape
    attn_w = n_heads * HEAD_WIDTH
    pool_w = w_in.shape[1] - 3 * attn_w
    tm = ATTN_TILE
    nt = s // tm
    return pl.pallas_call(
        functools.partial(_mix_in_kernel, n_heads=n_heads),
        grid=(nt,),
        in_specs=[
            pl.BlockSpec((tm, d), lambda i: (i, 0)),
            _resident((1, d)),
            _resident(w_in.shape),
            pl.BlockSpec((tm, HEAD_WIDTH), lambda i: (i, 0)),
            pl.BlockSpec((tm, HEAD_WIDTH), lambda i: (i, 0)),
        ],
        out_specs=[
            pl.BlockSpec((n_heads, 2, HEAD_WIDTH, tm), lambda i: (0, 0, 0, i)),
            pl.BlockSpec((tm, attn_w), lambda i: (i, 0)),
            pl.BlockSpec((n_heads, None, HEAD_WIDTH, tm), lambda i: (0, i, 0, 0)),
            pl.BlockSpec((tm, pool_w), lambda i: (i, 0)),
        ],
        out_shape=[
            jax.ShapeDtypeStruct((n_heads, 2, HEAD_WIDTH, s), BF16),
            jax.ShapeDtypeStruct((s, attn_w), BF16),
            jax.ShapeDtypeStruct((n_heads, nt, HEAD_WIDTH, tm), BF16),
            jax.ShapeDtypeStruct((s, pool_w), F32),
        ],
        compiler_params=_compiler_params(1),
        name="mix_in",
    )(x, g, w_in, cos, sin_signed)


def _attn_kernel(lam_ref, gain_ref, qz_ref, k_ref, vt_ref, o_ref, acc_ref, *, tile, lam_init):
    i = pl.program_id(1)
    acc_ref[...] = jnp.zeros_like(acc_ref)

    def step(j, carry, masked):
        kt = k_ref[pl.ds(pl.multiple_of(j * tile, tile), tile), :]
        vt = vt_ref[j]
        out = []
        for c in range(2):
            m, l = carry[2 * c], carry[2 * c + 1]
            s = jnp.dot(kt, qz_ref[c], preferred_element_type=F32)
            if masked:
                key = lax.broadcasted_iota(jnp.int32, s.shape, 0)
                qry = lax.broadcasted_iota(jnp.int32, s.shape, 1)
                s = jnp.where(key <= qry, s, -jnp.inf)
            m_new = jnp.maximum(m, jnp.max(s, axis=0, keepdims=True))
            alpha = jnp.exp(m - m_new)
            p = jnp.exp(s - m_new)
            l_new = alpha * l + jnp.sum(p, axis=0, keepdims=True)
            acc_ref[c] = alpha * acc_ref[c] + jnp.dot(vt, p.astype(BF16), preferred_element_type=F32)
            out += [m_new, l_new]
        return tuple(out)

    stat = (1, tile)
    init = (jnp.full(stat, NEG_INIT, F32), jnp.zeros(stat, F32)) * 2
    carry = lax.fori_loop(0, i, lambda j, c: step(j, c, False), init)
    _, l1, _, l2 = step(i, carry, True)

    lp = lam_ref[...]
    lam = (jnp.exp(jnp.sum(lp[0:1] * lp[1:2], axis=-1, keepdims=True))
           - jnp.exp(jnp.sum(lp[2:3] * lp[3:4], axis=-1, keepdims=True)) + lam_init)
    o = acc_ref[0] / l1 - lam * (acc_ref[1] / l2)
    ms = jnp.mean(o * o, axis=0, keepdims=True)
    y = o * lax.rsqrt(ms + NORM_EPS) * gain_ref[...] * (1.0 - lam_init)
    o_ref[...] = y.T.astype(BF16)


def _attention(lam_params, gain, qz, k, vt, lam_init):
    n_heads, _, _, s = qz.shape
    tile = vt.shape[-1]
    nt = s // tile
    return pl.pallas_call(
        functools.partial(_attn_kernel, tile=tile, lam_init=lam_init),
        grid=(n_heads, nt),
        in_specs=[
            _resident(lam_params.shape),
            _resident(gain.shape),
            pl.BlockSpec((None, 2, HEAD_WIDTH, tile), lambda h, i: (h, 0, 0, i)),
            pl.BlockSpec((s, HEAD_WIDTH), lambda h, i: (0, h)),
            pl.BlockSpec((None, nt, HEAD_WIDTH, tile), lambda h, i: (h, 0, 0, 0)),
        ],
        out_specs=pl.BlockSpec((tile, HEAD_WIDTH), lambda h, i: (i, h)),
        out_shape=jax.ShapeDtypeStruct((s, n_heads * HEAD_WIDTH), BF16),
        scratch_shapes=[pltpu.VMEM((2, HEAD_WIDTH, tile), F32)],
        compiler_params=_compiler_params(2),
        name="diff_attn",
    )(lam_params, gain, qz, k, vt)


def _mix_out_kernel(x_ref, o_ref, u_ref, halo_ref, pw_ref, ps_ref, wo_ref, out_ref):
    i = pl.program_id(0)
    u = u_ref[...]
    tm = u.shape[0]
    halo = jnp.where(i == 0, 0.0, halo_ref[...])
    ext = jnp.concatenate([halo, u], axis=0)
    pos = i * tm + lax.broadcasted_iota(jnp.int32, (tm, 1), 0)
    gw = pw_ref.shape[1]
    pooled = []
    for g, win in enumerate(POOL_WINDOWS):
        acc = ext[:, g * gw:(g + 1) * gw]
        span = 1
        while span < win:
            acc = acc + pltpu.roll(acc, span, 0)
            span *= 2
        count = jnp.minimum(pos + 1, win).astype(F32)
        diff = acc[POOL_HALO:] / count - u[:, g * gw:(g + 1) * gw]
        pooled.append(jnp.dot(diff.astype(BF16), pw_ref[g], preferred_element_type=F32))
    p = jnp.concatenate(pooled, axis=1) * ps_ref[...]
    mix = jnp.concatenate([o_ref[...], p.astype(BF16)], axis=1)
    out_ref[...] = x_ref[...] + jnp.dot(mix, wo_ref[...], preferred_element_type=F32)


def _mix_out(x, o, u, pool_w, pool_scale, w_out):
    s, d = x.shape
    tm = ROW_TILE
    halo_blocks = tm // POOL_HALO
    return pl.pallas_call(
        _mix_out_kernel,
        grid=(s // tm,),
        in_specs=[
            pl.BlockSpec((tm, d), lambda i: (i, 0)),
            pl.BlockSpec((tm, o.shape[1]), lambda i: (i, 0)),
            pl.BlockSpec((tm, u.shape[1]), lambda i: (i, 0)),
            pl.BlockSpec((POOL_HALO, u.shape[1]), lambda i: (jnp.maximum(i * halo_blocks - 1, 0), 0)),
            _resident(pool_w.shape),
            _resident(pool_scale.shape),
            _resident(w_out.shape),
        ],
        out_specs=pl.BlockSpec((tm, d), lambda i: (i, 0)),
        out_shape=jax.ShapeDtypeStruct((s, d), F32),
        compiler_params=_compiler_params(1),
        name="mix_out",
    )(x, o, u, u, pool_w, pool_scale, w_out)


def _final_norm_kernel(x_ref, g_ref, o_ref):
    o_ref[...] = _rms_norm(x_ref[...], g_ref[...])


def _final_norm(x, g):
    s, d = x.shape
    tm = ROW_TILE
    row = pl.BlockSpec((tm, d), lambda i: (i, 0))
    return pl.pallas_call(
        _final_norm_kernel,
        grid=(s // tm,),
        in_specs=[row, _resident((1, d))],
        out_specs=row,
        out_shape=jax.ShapeDtypeStruct((s, d), F32),
        compiler_params=_compiler_params(1),
        name="final_norm",
    )(x, g)


def _rope_tables(seq):
    inv_freq = ROPE_THETA ** (-jnp.arange(0, HEAD_DIM, 2, dtype=F32) / HEAD_DIM)
    ang = jnp.arange(seq, dtype=F32)[:, None] * inv_freq[None, :]
    cos = jnp.cos(ang)
    sin = jnp.sin(ang)
    return jnp.tile(cos, (1, 4)), jnp.tile(jnp.concatenate([-sin, sin], axis=1), (1, 2))


def kernel(x, ffn1_norm, ffn1_w_gate, ffn1_w_up, ffn1_w_down, mix_norm, w_in, lambda_q1, lambda_k1, lambda_q2, lambda_k2, subln_gain, pool_w, pool_scale, w_out, ffn2_norm, ffn2_w_gate, ffn2_w_up, ffn2_w_down, final_norm):
    b, s, d = x.shape
    depth = w_in.shape[0]
    attn_w = (w_in.shape[2] - w_out.shape[1]) // 2
    n_heads = attn_w // HEAD_WIDTH
    cos, sin_signed = _rope_tables(s)
    bf = lambda w: w.astype(BF16)
    outs = []
    for bi in range(b):
        xb = x[bi]
        for l in range(depth):
            xb = _ffn(xb, ffn1_norm[l][None], bf(ffn1_w_gate[l]), bf(ffn1_w_up[l]), bf(ffn1_w_down[l]))
            qz, k, vt, u = _mix_in(xb, mix_norm[l][None], bf(w_in[l]), cos, sin_signed, n_heads)
            lam_init = 0.8 - 0.6 * math.exp(-0.3 * l)
            lam_params = jnp.stack([lambda_q1[l], lambda_k1[l], lambda_q2[l], lambda_k2[l]]).astype(F32)
            o = _attention(lam_params, subln_gain[l].astype(F32)[:, None], qz, k, vt, lam_init)
            xb = _mix_out(xb, o, u, bf(pool_w[l]), pool_scale[l][None], bf(w_out[l]))
            xb = _ffn(xb, ffn2_norm[l][None], bf(ffn2_w_gate[l]), bf(ffn2_w_up[l]), bf(ffn2_w_down[l]))
        outs.append(_final_norm(xb, final_norm[None]))
    return jnp.stack(outs)
```

```python
import functools
import math

import jax
import jax.numpy as jnp
from jax import lax
from jax.experimental import pallas as pl
from jax.experimental.pallas import tpu as pltpu

F32 = jnp.float32
BF16 = jnp.bfloat16

HEAD_DIM = 64
HEAD_WIDTH = 2 * HEAD_DIM
POOL_WINDOWS = (2, 4, 8, 16)
V_ROWS = HEAD_WIDTH + 16
POOL_HALO = 16
ROPE_THETA = 10000.0
NORM_EPS = 1e-6
NEG_INIT = -1e30

VMEM_LIMIT_BYTES = 56 * 1024 * 1024
ROW_TILE = 512
ATTN_TILE = 512


def _rms_norm(x, g):
    ms = jnp.mean(x * x, axis=-1, keepdims=True)
    return x * lax.rsqrt(ms + NORM_EPS) * g


def _compiler_params(n_axes):
    return pltpu.CompilerParams(
        dimension_semantics=("arbitrary",) * n_axes,
        vmem_limit_bytes=VMEM_LIMIT_BYTES,
    )


def _resident(shape):
    return pl.BlockSpec(shape, lambda *_: (0,) * len(shape), pipeline_mode=pl.Buffered(1))


def _ffn_kernel(x_ref, g_ref, wg_ref, wu_ref, wd_ref, o_ref):
    x = x_ref[...]
    h = _rms_norm(x, g_ref[...]).astype(BF16)
    gate = jnp.dot(h, wg_ref[...], preferred_element_type=F32)
    up = jnp.dot(h, wu_ref[...], preferred_element_type=F32)
    act = (gate * jax.nn.sigmoid(gate) * up).astype(BF16)
    y = jnp.dot(act, wd_ref[...], preferred_element_type=F32)
    o_ref[...] = x + 0.5 * y


def _ffn(x, g, wg, wu, wd):
    s, d = x.shape
    f = wg.shape[1]
    tm = ROW_TILE
    row = pl.BlockSpec((tm, d), lambda i: (i, 0))
    return pl.pallas_call(
        _ffn_kernel,
        grid=(s // tm,),
        in_specs=[row, _resident((1, d)), _resident((d, f)), _resident((d, f)), _resident((f, d))],
        out_specs=row,
        out_shape=jax.ShapeDtypeStruct((s, d), F32),
        compiler_params=_compiler_params(1),
        name="ffn",
    )(x, g, wg, wu, wd)


def _rope(t, cos, sin_signed, first_half):
    rot = jnp.where(first_half, pltpu.roll(t, HEAD_WIDTH - HEAD_DIM // 2, 1),
                    pltpu.roll(t, HEAD_DIM // 2, 1))
    return t * cos + rot * sin_signed


def _mix_in_kernel(x_ref, g_ref, w_ref, cos_ref, sin_ref, qz_ref, k_ref, vt_ref, u_ref, *, n_heads):
    attn_w = n_heads * HEAD_WIDTH
    h = _rms_norm(x_ref[...], g_ref[...]).astype(BF16)
    proj = jnp.dot(h, w_ref[...], preferred_element_type=F32)
    cos = cos_ref[...]
    sin_signed = sin_ref[...]
    tm = cos.shape[0]
    lane = lax.broadcasted_iota(jnp.int32, (tm, HEAD_WIDTH), 1)
    first_half = (lane % HEAD_DIM) < HEAD_DIM // 2
    map_row = lax.broadcasted_iota(jnp.int32, (HEAD_WIDTH, tm), 0)
    scale = HEAD_DIM ** -0.5 * math.log2(math.e)
    ones_rows = (lax.broadcasted_iota(jnp.int32, (V_ROWS - HEAD_WIDTH, tm), 0) == 0).astype(BF16)
    for hh in range(n_heads):
        c0 = hh * HEAD_WIDTH
        q = _rope(proj[:, c0:c0 + HEAD_WIDTH], cos, sin_signed, first_half) * scale
        qt = q.T
        qz_ref[hh, 0] = jnp.where(map_row < HEAD_DIM, qt, 0.0).astype(BF16)
        qz_ref[hh, 1] = jnp.where(map_row >= HEAD_DIM, qt, 0.0).astype(BF16)
        k = _rope(proj[:, attn_w + c0:attn_w + c0 + HEAD_WIDTH], cos, sin_signed, first_half)
        k_ref[:, c0:c0 + HEAD_WIDTH] = k.astype(BF16)
        v = proj[:, 2 * attn_w + c0:2 * attn_w + c0 + HEAD_WIDTH]
        vt_ref[hh, :HEAD_WIDTH] = v.T.astype(BF16)
        vt_ref[hh, HEAD_WIDTH:] = ones_rows
    u_ref[...] = proj[:, 3 * attn_w:]


def _mix_in(x, g, w_in, cos, sin_signed, n_heads):
    s, d = x.shape
    attn_w = n_heads * HEAD_WIDTH
    pool_w = w_in.shape[1] - 3 * attn_w
    tm = ATTN_TILE
    nt = s // tm
    return pl.pallas_call(
        functools.partial(_mix_in_kernel, n_heads=n_heads),
        grid=(nt,),
        in_specs=[
            pl.BlockSpec((tm, d), lambda i: (i, 0)),
            _resident((1, d)),
            _resident(w_in.shape),
            pl.BlockSpec((tm, HEAD_WIDTH), lambda i: (i, 0)),
            pl.BlockSpec((tm, HEAD_WIDTH), lambda i: (i, 0)),
        ],
        out_specs=[
            pl.BlockSpec((n_heads, 2, HEAD_WIDTH, tm), lambda i: (0, 0, 0, i)),
            pl.BlockSpec((tm, attn_w), lambda i: (i, 0)),
            pl.BlockSpec((n_heads, None, V_ROWS, tm), lambda i: (0, i, 0, 0)),
            pl.BlockSpec((tm, pool_w), lambda i: (i, 0)),
        ],
        out_shape=[
            jax.ShapeDtypeStruct((n_heads, 2, HEAD_WIDTH, s), BF16),
            jax.ShapeDtypeStruct((s, attn_w), BF16),
            jax.ShapeDtypeStruct((n_heads, nt, V_ROWS, tm), BF16),
            jax.ShapeDtypeStruct((s, pool_w), F32),
        ],
        compiler_params=_compiler_params(1),
        name="mix_in",
    )(x, g, w_in, cos, sin_signed)


def _attn_kernel(lam_ref, gain_ref, qz_ref, k_ref, vt_ref, o_ref, acc_ref, s_ref, m_ref, mt_ref, *, tile, lam_init):
    i = pl.program_id(1)
    acc_ref[...] = jnp.zeros_like(acc_ref)
    m_ref[...] = jnp.full_like(m_ref, NEG_INIT)

    def scores(j, slot, c, masked=False):
        kt = k_ref[pl.ds(pl.multiple_of(j * tile, tile), tile), :]
        s = jnp.dot(kt, qz_ref[c], preferred_element_type=F32)
        if masked:
            key = lax.broadcasted_iota(jnp.int32, s.shape, 0)
            qry = lax.broadcasted_iota(jnp.int32, s.shape, 1)
            s = jnp.where(key <= qry, s, -jnp.inf)
        s_ref[slot, c] = s
        mt_ref[slot, c] = jnp.max(s, axis=0, keepdims=True)

    def softmax_pv(j, slot, c):
        m = m_ref[c]
        m_new = jnp.maximum(m, mt_ref[slot, c])
        alpha = jnp.exp2(m - m_new)
        p = jnp.exp2(s_ref[slot, c] - m_new).astype(BF16)
        acc_ref[c] = alpha * acc_ref[c] + jnp.dot(vt_ref[j], p, preferred_element_type=F32)
        m_ref[c] = m_new

    def stage(cur, cur_slot, nxt):
        for c in range(2):
            if nxt is not None:
                scores(nxt, 1 - cur_slot, c)
            softmax_pv(cur, cur_slot, c)

    for c in range(2):
        scores(i, 0, c, masked=True)

    def pair(t, carry):
        stage(jnp.where(t == 0, i, 2 * t - 1), 0, 2 * t)
        stage(2 * t, 1, 2 * t + 1)
        return carry

    lax.fori_loop(0, i // 2, pair, 0)
    last = jnp.maximum(i - 1, 0)

    @pl.when(i % 2 == 0)
    def _():
        stage(last, 0, None)

    @pl.when(i % 2 == 1)
    def _():
        stage(jnp.where(i == 1, i, i - 2), 0, last)
        stage(last, 1, None)

    lp = lam_ref[...]
    lam = (jnp.exp(jnp.sum(lp[0:1] * lp[1:2], axis=-1, keepdims=True))
           - jnp.exp(jnp.sum(lp[2:3] * lp[3:4], axis=-1, keepdims=True)) + lam_init)
    a1 = acc_ref[0]
    a2 = acc_ref[1]
    o = (a1[:HEAD_WIDTH] / a1[HEAD_WIDTH:HEAD_WIDTH + 1]
         - lam * (a2[:HEAD_WIDTH] / a2[HEAD_WIDTH:HEAD_WIDTH + 1]))
    ms = jnp.mean(o * o, axis=0, keepdims=True)
    y = o * lax.rsqrt(ms + NORM_EPS) * gain_ref[...] * (1.0 - lam_init)
    o_ref[...] = y.T.astype(BF16)


def _attention(lam_params, gain, qz, k, vt, lam_init):
    n_heads, _, _, s = qz.shape
    v_rows, tile = vt.shape[-2:]
    nt = s // tile
    return pl.pallas_call(
        functools.partial(_attn_kernel, tile=tile, lam_init=lam_init),
        grid=(n_heads, nt),
        in_specs=[
            _resident(lam_params.shape),
            _resident(gain.shape),
            pl.BlockSpec((None, 2, HEAD_WIDTH, tile), lambda h, i: (h, 0, 0, i)),
            pl.BlockSpec((s, HEAD_WIDTH), lambda h, i: (0, h)),
            pl.BlockSpec((None, nt, v_rows, tile), lambda h, i: (h, 0, 0, 0)),
        ],
        out_specs=pl.BlockSpec((tile, HEAD_WIDTH), lambda h, i: (i, h)),
        out_shape=jax.ShapeDtypeStruct((s, n_heads * HEAD_WIDTH), BF16),
        scratch_shapes=[
            pltpu.VMEM((2, v_rows, tile), F32),
            pltpu.VMEM((2, 2, tile, tile), F32),
            pltpu.VMEM((2, 1, tile), F32),
            pltpu.VMEM((2, 2, 1, tile), F32),
        ],
        compiler_params=_compiler_params(2),
        name="diff_attn",
    )(lam_params, gain, qz, k, vt)


def _mix_out_kernel(x_ref, o_ref, u_ref, halo_ref, pw_ref, ps_ref, wo_ref, out_ref):
    i = pl.program_id(0)
    u = u_ref[...]
    tm = u.shape[0]
    halo = jnp.where(i == 0, 0.0, halo_ref[...])
    ext = jnp.concatenate([halo, u], axis=0)
    pos = i * tm + lax.broadcasted_iota(jnp.int32, (tm, 1), 0)
    gw = pw_ref.shape[1]
    pooled = []
    for g, win in enumerate(POOL_WINDOWS):
        acc = ext[:, g * gw:(g + 1) * gw]
        span = 1
        while span < win:
            acc = acc + pltpu.roll(acc, span, 0)
            span *= 2
        count = jnp.minimum(pos + 1, win).astype(F32)
        diff = acc[POOL_HALO:] / count - u[:, g * gw:(g + 1) * gw]
        pooled.append(jnp.dot(diff.astype(BF16), pw_ref[g], preferred_element_type=F32))
    p = jnp.concatenate(pooled, axis=1) * ps_ref[...]
    mix = jnp.concatenate([o_ref[...], p.astype(BF16)], axis=1)
    out_ref[...] = x_ref[...] + jnp.dot(mix, wo_ref[...], preferred_element_type=F32)


def _mix_out(x, o, u, pool_w, pool_scale, w_out):
    s, d = x.shape
    tm = ROW_TILE
    halo_blocks = tm // POOL_HALO
    return pl.pallas_call(
        _mix_out_kernel,
        grid=(s // tm,),
        in_specs=[
            pl.BlockSpec((tm, d), lambda i: (i, 0)),
            pl.BlockSpec((tm, o.shape[1]), lambda i: (i, 0)),
            pl.BlockSpec((tm, u.shape[1]), lambda i: (i, 0)),
            pl.BlockSpec((POOL_HALO, u.shape[1]), lambda i: (jnp.maximum(i * halo_blocks - 1, 0), 0)),
            _resident(pool_w.shape),
            _resident(pool_scale.shape),
            _resident(w_out.shape),
        ],
        out_specs=pl.BlockSpec((tm, d), lambda i: (i, 0)),
        out_shape=jax.ShapeDtypeStruct((s, d), F32),
        compiler_params=_compiler_params(1),
        name="mix_out",
    )(x, o, u, u, pool_w, pool_scale, w_out)


def _final_norm_kernel(x_ref, g_ref, o_ref):
    o_ref[...] = _rms_norm(x_ref[...], g_ref[...])


def _final_norm(x, g):
    s, d = x.shape
    tm = ROW_TILE
    row = pl.BlockSpec((tm, d), lambda i: (i, 0))
    return pl.pallas_call(
        _final_norm_kernel,
        grid=(s // tm,),
        in_specs=[row, _resident((1, d))],
        out_specs=row,
        out_shape=jax.ShapeDtypeStruct((s, d), F32),
        compiler_params=_compiler_params(1),
        name="final_norm",
    )(x, g)


def _rope_tables(seq):
    inv_freq = ROPE_THETA ** (-jnp.arange(0, HEAD_DIM, 2, dtype=F32) / HEAD_DIM)
    ang = jnp.arange(seq, dtype=F32)[:, None] * inv_freq[None, :]
    cos = jnp.cos(ang)
    sin = jnp.sin(ang)
    return jnp.tile(cos, (1, 4)), jnp.tile(jnp.concatenate([-sin, sin], axis=1), (1, 2))


def kernel(x, ffn1_norm, ffn1_w_gate, ffn1_w_up, ffn1_w_down, mix_norm, w_in, lambda_q1, lambda_k1, lambda_q2, lambda_k2, subln_gain, pool_w, pool_scale, w_out, ffn2_norm, ffn2_w_gate, ffn2_w_up, ffn2_w_down, final_norm):
    b, s, d = x.shape
    depth = w_in.shape[0]
    attn_w = (w_in.shape[2] - w_out.shape[1]) // 2
    n_heads = attn_w // HEAD_WIDTH
    cos, sin_signed = _rope_tables(s)
    bf = lambda w: w.astype(BF16)
    outs = []
    for bi in range(b):
        xb = x[bi]
        for l in range(depth):
            xb = _ffn(xb, ffn1_norm[l][None], bf(ffn1_w_gate[l]), bf(ffn1_w_up[l]), bf(ffn1_w_down[l]))
            qz, k, vt, u = _mix_in(xb, mix_norm[l][None], bf(w_in[l]), cos, sin_signed, n_heads)
            lam_init = 0.8 - 0.6 * math.exp(-0.3 * l)
            lam_params = jnp.stack([lambda_q1[l], lambda_k1[l], lambda_q2[l], lambda_k2[l]]).astype(F32)
            o = _attention(lam_params, subln_gain[l].astype(F32)[:, None], qz, k, vt, lam_init)
            xb = _mix_out(xb, o, u, bf(pool_w[l]), pool_scale[l][None], bf(w_out[l]))
            xb = _ffn(xb, ffn2_norm[l][None], bf(ffn2_w_gate[l]), bf(ffn2_w_up[l]), bf(ffn2_w_down[l]))
        outs.append(_final_norm(xb, final_norm[None]))
    return jnp.stack(outs)
```

```python
import functools
import math

import jax
import jax.numpy as jnp
from jax import lax
from jax.experimental import pallas as pl
from jax.experimental.pallas import tpu as pltpu

F32 = jnp.float32
BF16 = jnp.bfloat16

HEAD_DIM = 64
HEAD_WIDTH = 2 * HEAD_DIM
POOL_WINDOWS = (2, 4, 8, 16)
V_ROWS = HEAD_WIDTH + 16
POOL_HALO = 16
ROPE_THETA = 10000.0
NORM_EPS = 1e-6
NEG_INIT = -1e30

VMEM_LIMIT_BYTES = 56 * 1024 * 1024
ROW_TILE = 512
ATTN_TILE = 512


def _rms_norm(x, g):
    ms = jnp.mean(x * x, axis=-1, keepdims=True)
    return x * lax.rsqrt(ms + NORM_EPS) * g


def _compiler_params(n_axes):
    return pltpu.CompilerParams(
        dimension_semantics=("arbitrary",) * n_axes,
        vmem_limit_bytes=VMEM_LIMIT_BYTES,
    )


def _resident(shape):
    return pl.BlockSpec(shape, lambda *_: (0,) * len(shape), pipeline_mode=pl.Buffered(1))


def _ffn_kernel(x_ref, g_ref, wg_ref, wu_ref, wd_ref, o_ref):
    x = x_ref[...]
    h = _rms_norm(x, g_ref[...]).astype(BF16)
    gate = jnp.dot(h, wg_ref[...], preferred_element_type=F32)
    up = jnp.dot(h, wu_ref[...], preferred_element_type=F32)
    act = (gate * jax.nn.sigmoid(gate) * up).astype(BF16)
    y = jnp.dot(act, wd_ref[...], preferred_element_type=F32)
    o_ref[...] = x + 0.5 * y


def _ffn(x, g, wg, wu, wd):
    s, d = x.shape
    f = wg.shape[1]
    tm = ROW_TILE
    row = pl.BlockSpec((tm, d), lambda i: (i, 0))
    return pl.pallas_call(
        _ffn_kernel,
        grid=(s // tm,),
        in_specs=[row, _resident((1, d)), _resident((d, f)), _resident((d, f)), _resident((f, d))],
        out_specs=row,
        out_shape=jax.ShapeDtypeStruct((s, d), F32),
        compiler_params=_compiler_params(1),
        name="ffn",
    )(x, g, wg, wu, wd)


def _rope(t, cos, sin_signed, first_half):
    rot = jnp.where(first_half, pltpu.roll(t, HEAD_WIDTH - HEAD_DIM // 2, 1),
                    pltpu.roll(t, HEAD_DIM // 2, 1))
    return t * cos + rot * sin_signed


def _mix_in_kernel(x_ref, g_ref, w_ref, cos_ref, sin_ref, qz_ref, k_ref, vt_ref, u_ref, *, n_heads):
    attn_w = n_heads * HEAD_WIDTH
    h = _rms_norm(x_ref[...], g_ref[...]).astype(BF16)
    q_all, k_all, v_all, u_all = (
        jnp.dot(h, w_ref[:, lo:hi], preferred_element_type=F32)
        for lo, hi in ((0, attn_w), (attn_w, 2 * attn_w), (2 * attn_w, 3 * attn_w), (3 * attn_w, w_ref.shape[1])))
    cos = cos_ref[...]
    sin_signed = sin_ref[...]
    tm = cos.shape[0]
    lane = lax.broadcasted_iota(jnp.int32, (tm, HEAD_WIDTH), 1)
    first_half = (lane % HEAD_DIM) < HEAD_DIM // 2
    map_row = lax.broadcasted_iota(jnp.int32, (HEAD_WIDTH, tm), 0)
    scale = HEAD_DIM ** -0.5 * math.log2(math.e)
    ones_rows = (lax.broadcasted_iota(jnp.int32, (V_ROWS - HEAD_WIDTH, tm), 0) == 0).astype(BF16)
    for hh in range(n_heads):
        c0 = hh * HEAD_WIDTH
        q = _rope(q_all[:, c0:c0 + HEAD_WIDTH], cos, sin_signed, first_half) * scale
        qt = q.T
        qz_ref[hh, 0] = jnp.where(map_row < HEAD_DIM, qt, 0.0).astype(BF16)
        qz_ref[hh, 1] = jnp.where(map_row >= HEAD_DIM, qt, 0.0).astype(BF16)
        k = _rope(k_all[:, c0:c0 + HEAD_WIDTH], cos, sin_signed, first_half)
        k_ref[:, c0:c0 + HEAD_WIDTH] = k.astype(BF16)
        v = v_all[:, c0:c0 + HEAD_WIDTH]
        vt_ref[hh, :HEAD_WIDTH] = v.T.astype(BF16)
        vt_ref[hh, HEAD_WIDTH:] = ones_rows
    u_ref[...] = u_all


def _mix_in(x, g, w_in, cos, sin_signed, n_heads):
    s, d = x.shape
    attn_w = n_heads * HEAD_WIDTH
    pool_w = w_in.shape[1] - 3 * attn_w
    tm = ATTN_TILE
    nt = s // tm
    return pl.pallas_call(
        functools.partial(_mix_in_kernel, n_heads=n_heads),
        grid=(nt,),
        in_specs=[
            pl.BlockSpec((tm, d), lambda i: (i, 0)),
            _resident((1, d)),
            _resident(w_in.shape),
            pl.BlockSpec((tm, HEAD_WIDTH), lambda i: (i, 0)),
            pl.BlockSpec((tm, HEAD_WIDTH), lambda i: (i, 0)),
        ],
        out_specs=[
            pl.BlockSpec((n_heads, 2, HEAD_WIDTH, tm), lambda i: (0, 0, 0, i)),
            pl.BlockSpec((tm, attn_w), lambda i: (i, 0)),
            pl.BlockSpec((n_heads, None, V_ROWS, tm), lambda i: (0, i, 0, 0)),
            pl.BlockSpec((tm, pool_w), lambda i: (i, 0)),
        ],
        out_shape=[
            jax.ShapeDtypeStruct((n_heads, 2, HEAD_WIDTH, s), BF16),
            jax.ShapeDtypeStruct((s, attn_w), BF16),
            jax.ShapeDtypeStruct((n_heads, nt, V_ROWS, tm), BF16),
            jax.ShapeDtypeStruct((s, pool_w), F32),
        ],
        compiler_params=_compiler_params(1),
        name="mix_in",
    )(x, g, w_in, cos, sin_signed)


def _attn_kernel(lam_ref, gain_ref, qz_ref, k_ref, vt_ref, o_ref, acc_ref, s_ref, m_ref, mt_ref, *, tile, lam_init):
    i = pl.program_id(1)
    acc_ref[...] = jnp.zeros_like(acc_ref)
    m_ref[...] = jnp.full_like(m_ref, NEG_INIT)

    def scores(j, slot, c, masked=False):
        kt = k_ref[pl.ds(pl.multiple_of(j * tile, tile), tile), :]
        s = jnp.dot(kt, qz_ref[c], preferred_element_type=F32)
        if masked:
            key = lax.broadcasted_iota(jnp.int32, s.shape, 0)
            qry = lax.broadcasted_iota(jnp.int32, s.shape, 1)
            s = jnp.where(key <= qry, s, -jnp.inf)
        s_ref[slot, c] = s
        mt_ref[slot, c] = jnp.max(s, axis=0, keepdims=True)

    def softmax_pv(j, slot, c):
        m = m_ref[c]
        m_new = jnp.maximum(m, mt_ref[slot, c])
        alpha = jnp.exp2(m - m_new)
        p = jnp.exp2(s_ref[slot, c] - m_new).astype(BF16)
        acc_ref[c] = alpha * acc_ref[c] + jnp.dot(vt_ref[j], p, preferred_element_type=F32)
        m_ref[c] = m_new

    def stage(cur, cur_slot, nxt):
        for c in range(2):
            if nxt is not None:
                scores(nxt, 1 - cur_slot, c)
            softmax_pv(cur, cur_slot, c)

    for c in range(2):
        scores(i, 0, c, masked=True)

    def pair(t, carry):
        stage(jnp.where(t == 0, i, 2 * t - 1), 0, 2 * t)
        stage(2 * t, 1, 2 * t + 1)
        return carry

    def quad(q, carry):
        return pair(2 * q + 1, pair(2 * q, carry))

    def octet(e, carry):
        return quad(2 * e + 1, quad(2 * e, carry))

    lax.fori_loop(0, i // 8, octet, 0)
    lax.fori_loop(2 * (i // 8), i // 4, quad, 0)
    lax.fori_loop(2 * (i // 4), i // 2, pair, 0)
    last = jnp.maximum(i - 1, 0)

    @pl.when(i % 2 == 0)
    def _():
        stage(last, 0, None)

    @pl.when(i % 2 == 1)
    def _():
        stage(jnp.where(i == 1, i, i - 2), 0, last)
        stage(last, 1, None)

    lp = lam_ref[...]
    lam = (jnp.exp(jnp.sum(lp[0:1] * lp[1:2], axis=-1, keepdims=True))
           - jnp.exp(jnp.sum(lp[2:3] * lp[3:4], axis=-1, keepdims=True)) + lam_init)
    a1 = acc_ref[0]
    a2 = acc_ref[1]
    o = (a1[:HEAD_WIDTH] / a1[HEAD_WIDTH:HEAD_WIDTH + 1]
         - lam * (a2[:HEAD_WIDTH] / a2[HEAD_WIDTH:HEAD_WIDTH + 1]))
    ms = jnp.mean(o * o, axis=0, keepdims=True)
    y = o * lax.rsqrt(ms + NORM_EPS) * gain_ref[...] * (1.0 - lam_init)
    o_ref[...] = y.T.astype(BF16)


def _attention(lam_params, gain, qz, k, vt, lam_init):
    n_heads, _, _, s = qz.shape
    v_rows, tile = vt.shape[-2:]
    nt = s // tile
    return pl.pallas_call(
        functools.partial(_attn_kernel, tile=tile, lam_init=lam_init),
        grid=(n_heads, nt),
        in_specs=[
            _resident(lam_params.shape),
            _resident(gain.shape),
            pl.BlockSpec((None, 2, HEAD_WIDTH, tile), lambda h, i: (h, 0, 0, i)),
            pl.BlockSpec((s, HEAD_WIDTH), lambda h, i: (0, h)),
            pl.BlockSpec((None, nt, v_rows, tile), lambda h, i: (h, 0, 0, 0)),
        ],
        out_specs=pl.BlockSpec((tile, HEAD_WIDTH), lambda h, i: (i, h)),
        out_shape=jax.ShapeDtypeStruct((s, n_heads * HEAD_WIDTH), BF16),
        scratch_shapes=[
            pltpu.VMEM((2, v_rows, tile), F32),
            pltpu.VMEM((2, 2, tile, tile), F32),
            pltpu.VMEM((2, 1, tile), F32),
            pltpu.VMEM((2, 2, 1, tile), F32),
        ],
        compiler_params=_compiler_params(2),
        name="diff_attn",
    )(lam_params, gain, qz, k, vt)


def _mix_out_kernel(x_ref, o_ref, u_ref, halo_ref, pw_ref, ps_ref, wo_ref, out_ref):
    i = pl.program_id(0)
    attn_w = o_ref.shape[1]
    y_attn = jnp.dot(o_ref[...], wo_ref[:attn_w, :], preferred_element_type=F32)
    u = u_ref[...]
    tm = u.shape[0]
    halo = jnp.where(i == 0, 0.0, halo_ref[...])
    ext = jnp.concatenate([halo, u], axis=0)
    pos = i * tm + lax.broadcasted_iota(jnp.int32, (tm, 1), 0)
    gw = pw_ref.shape[1]
    pooled = []
    for g, win in enumerate(POOL_WINDOWS):
        acc = ext[:, g * gw:(g + 1) * gw]
        span = 1
        while span < win:
            acc = acc + pltpu.roll(acc, span, 0)
            span *= 2
        count = jnp.minimum(pos + 1, win).astype(F32)
        diff = acc[POOL_HALO:] / count - u[:, g * gw:(g + 1) * gw]
        pooled.append(jnp.dot(diff.astype(BF16), pw_ref[g], preferred_element_type=F32))
    p = (jnp.concatenate(pooled, axis=1) * ps_ref[...]).astype(BF16)
    y_pool = jnp.dot(p, wo_ref[attn_w:, :], preferred_element_type=F32)
    out_ref[...] = x_ref[...] + (y_attn + y_pool)


def _mix_out(x, o, u, pool_w, pool_scale, w_out):
    s, d = x.shape
    tm = ROW_TILE
    halo_blocks = tm // POOL_HALO
    return pl.pallas_call(
        _mix_out_kernel,
        grid=(s // tm,),
        in_specs=[
            pl.BlockSpec((tm, d), lambda i: (i, 0)),
            pl.BlockSpec((tm, o.shape[1]), lambda i: (i, 0)),
            pl.BlockSpec((tm, u.shape[1]), lambda i: (i, 0)),
            pl.BlockSpec((POOL_HALO, u.shape[1]), lambda i: (jnp.maximum(i * halo_blocks - 1, 0), 0)),
            _resident(pool_w.shape),
            _resident(pool_scale.shape),
            _resident(w_out.shape),
        ],
        out_specs=pl.BlockSpec((tm, d), lambda i: (i, 0)),
        out_shape=jax.ShapeDtypeStruct((s, d), F32),
        compiler_params=_compiler_params(1),
        name="mix_out",
    )(x, o, u, u, pool_w, pool_scale, w_out)


def _final_norm_kernel(x_ref, g_ref, o_ref):
    o_ref[...] = _rms_norm(x_ref[...], g_ref[...])


def _final_norm(x, g):
    s, d = x.shape
    tm = ROW_TILE
    row = pl.BlockSpec((tm, d), lambda i: (i, 0))
    return pl.pallas_call(
        _final_norm_kernel,
        grid=(s // tm,),
        in_specs=[row, _resident((1, d))],
        out_specs=row,
        out_shape=jax.ShapeDtypeStruct((s, d), F32),
        compiler_params=_compiler_params(1),
        name="final_norm",
    )(x, g)


def _rope_tables(seq):
    inv_freq = ROPE_THETA ** (-jnp.arange(0, HEAD_DIM, 2, dtype=F32) / HEAD_DIM)
    ang = jnp.arange(seq, dtype=F32)[:, None] * inv_freq[None, :]
    cos = jnp.cos(ang)
    sin = jnp.sin(ang)
    return jnp.tile(cos, (1, 4)), jnp.tile(jnp.concatenate([-sin, sin], axis=1), (1, 2))


def kernel(x, ffn1_norm, ffn1_w_gate, ffn1_w_up, ffn1_w_down, mix_norm, w_in, lambda_q1, lambda_k1, lambda_q2, lambda_k2, subln_gain, pool_w, pool_scale, w_out, ffn2_norm, ffn2_w_gate, ffn2_w_up, ffn2_w_down, final_norm):
    b, s, d = x.shape
    depth = w_in.shape[0]
    attn_w = (w_in.shape[2] - w_out.shape[1]) // 2
    n_heads = attn_w // HEAD_WIDTH
    cos, sin_signed = _rope_tables(s)
    bf = lambda w: w.astype(BF16)
    assert b == 1, "the sequence kernels take one sequence; the problem fixes batch at 1"
    xb = x.reshape(s, d)
    for l in range(depth):
        xb = _ffn(xb, ffn1_norm[l][None], bf(ffn1_w_gate[l]), bf(ffn1_w_up[l]), bf(ffn1_w_down[l]))
        qz, k, vt, u = _mix_in(xb, mix_norm[l][None], bf(w_in[l]), cos, sin_signed, n_heads)
        lam_init = 0.8 - 0.6 * math.exp(-0.3 * l)
        lam_params = jnp.stack([lambda_q1[l], lambda_k1[l], lambda_q2[l], lambda_k2[l]]).astype(F32)
        o = _attention(lam_params, subln_gain[l].astype(F32)[:, None], qz, k, vt, lam_init)
        xb = _mix_out(xb, o, u, bf(pool_w[l]), pool_scale[l][None], bf(w_out[l]))
        xb = _ffn(xb, ffn2_norm[l][None], bf(ffn2_w_gate[l]), bf(ffn2_w_up[l]), bf(ffn2_w_down[l]))
    return _final_norm(xb, final_norm[None]).reshape(b, s, d)
```

```python
import functools
import math

import jax
import jax.numpy as jnp
from jax import lax
from jax.experimental import pallas as pl
from jax.experimental.pallas import tpu as pltpu

F32 = jnp.float32
BF16 = jnp.bfloat16

HEAD_DIM = 64
HEAD_WIDTH = 2 * HEAD_DIM
POOL_WINDOWS = (2, 4, 8, 16)
V_ROWS = HEAD_WIDTH + 16
POOL_HALO = 16
ROPE_THETA = 10000.0
NORM_EPS = 1e-6
NEG_INIT = -1e30

VMEM_LIMIT_BYTES = 56 * 1024 * 1024
ROW_TILE = 512
ATTN_TILE = 512


def _rms_norm(x, g):
    ms = jnp.mean(x * x, axis=-1, keepdims=True)
    return x * lax.rsqrt(ms + NORM_EPS) * g


def _compiler_params(n_axes):
    return pltpu.CompilerParams(
        dimension_semantics=("arbitrary",) * n_axes,
        vmem_limit_bytes=VMEM_LIMIT_BYTES,
    )


def _resident(shape):
    return pl.BlockSpec(shape, lambda *_: (0,) * len(shape), pipeline_mode=pl.Buffered(1))


def _swiglu_half_step(x, g_ref, wg_ref, wu_ref, wd_ref):
    h = _rms_norm(x, g_ref[...]).astype(BF16)
    gate = jnp.dot(h, wg_ref[...], preferred_element_type=F32)
    up = jnp.dot(h, wu_ref[...], preferred_element_type=F32)
    act = (gate * jax.nn.sigmoid(gate) * up).astype(BF16)
    y = jnp.dot(act, wd_ref[...], preferred_element_type=F32)
    return x + 0.5 * y


def _rope(t, cos, sin_signed, first_half):
    rot = jnp.where(first_half, pltpu.roll(t, HEAD_WIDTH - HEAD_DIM // 2, 1),
                    pltpu.roll(t, HEAD_DIM // 2, 1))
    return t * cos + rot * sin_signed


def _ffn_mix_in_kernel(x_ref, fg_ref, wg_ref, wu_ref, wd_ref, g_ref, w_ref, cos_ref, sin_ref,
                       x1_ref, qz_ref, k_ref, vt_ref, u_ref, *, n_heads):
    attn_w = n_heads * HEAD_WIDTH
    x1 = _swiglu_half_step(x_ref[...], fg_ref, wg_ref, wu_ref, wd_ref)
    x1_ref[...] = x1
    h = _rms_norm(x1, g_ref[...]).astype(BF16)
    q_all, k_all, v_all, u_all = (
        jnp.dot(h, w_ref[:, lo:hi], preferred_element_type=F32)
        for lo, hi in ((0, attn_w), (attn_w, 2 * attn_w), (2 * attn_w, 3 * attn_w), (3 * attn_w, w_ref.shape[1])))
    cos = cos_ref[...]
    sin_signed = sin_ref[...]
    tm = cos.shape[0]
    lane = lax.broadcasted_iota(jnp.int32, (tm, HEAD_WIDTH), 1)
    first_half = (lane % HEAD_DIM) < HEAD_DIM // 2
    map_row = lax.broadcasted_iota(jnp.int32, (HEAD_WIDTH, tm), 0)
    scale = HEAD_DIM ** -0.5 * math.log2(math.e)
    ones_rows = (lax.broadcasted_iota(jnp.int32, (V_ROWS - HEAD_WIDTH, tm), 0) == 0).astype(BF16)
    for hh in range(n_heads):
        c0 = hh * HEAD_WIDTH
        q = _rope(q_all[:, c0:c0 + HEAD_WIDTH], cos, sin_signed, first_half) * scale
        qt = q.T
        qz_ref[hh, 0] = jnp.where(map_row < HEAD_DIM, qt, 0.0).astype(BF16)
        qz_ref[hh, 1] = jnp.where(map_row >= HEAD_DIM, qt, 0.0).astype(BF16)
        k = _rope(k_all[:, c0:c0 + HEAD_WIDTH], cos, sin_signed, first_half)
        k_ref[:, c0:c0 + HEAD_WIDTH] = k.astype(BF16)
        v = v_all[:, c0:c0 + HEAD_WIDTH]
        vt_ref[hh, :HEAD_WIDTH] = v.T.astype(BF16)
        vt_ref[hh, HEAD_WIDTH:] = ones_rows
    u_ref[...] = u_all


def _ffn_mix_in(x, ffn_g, wg, wu, wd, g, w_in, cos, sin_signed, n_heads):
    s, d = x.shape
    attn_w = n_heads * HEAD_WIDTH
    pool_w = w_in.shape[1] - 3 * attn_w
    tm = ATTN_TILE
    nt = s // tm
    return pl.pallas_call(
        functools.partial(_ffn_mix_in_kernel, n_heads=n_heads),
        grid=(nt,),
        in_specs=[
            pl.BlockSpec((tm, d), lambda i: (i, 0)),
            _resident(ffn_g.shape),
            _resident(wg.shape),
            _resident(wu.shape),
            _resident(wd.shape),
            _resident(g.shape),
            _resident(w_in.shape),
            pl.BlockSpec((tm, HEAD_WIDTH), lambda i: (i, 0)),
            pl.BlockSpec((tm, HEAD_WIDTH), lambda i: (i, 0)),
        ],
        out_specs=[
            pl.BlockSpec((tm, d), lambda i: (i, 0)),
            pl.BlockSpec((n_heads, 2, HEAD_WIDTH, tm), lambda i: (0, 0, 0, i)),
            pl.BlockSpec((tm, attn_w), lambda i: (i, 0)),
            pl.BlockSpec((n_heads, None, V_ROWS, tm), lambda i: (0, i, 0, 0)),
            pl.BlockSpec((tm, pool_w), lambda i: (i, 0)),
        ],
        out_shape=[
            jax.ShapeDtypeStruct((s, d), F32),
            jax.ShapeDtypeStruct((n_heads, 2, HEAD_WIDTH, s), BF16),
            jax.ShapeDtypeStruct((s, attn_w), BF16),
            jax.ShapeDtypeStruct((n_heads, nt, V_ROWS, tm), BF16),
            jax.ShapeDtypeStruct((s, pool_w), F32),
        ],
        compiler_params=_compiler_params(1),
        name="ffn_mix_in",
    )(x, ffn_g, wg, wu, wd, g, w_in, cos, sin_signed)


def _attn_kernel(lam_ref, gain_ref, qz_ref, k_ref, vt_ref, o_ref, acc_ref, s_ref, m_ref, mt_ref, *, tile, lam_init):
    i = pl.program_id(1)
    acc_ref[...] = jnp.zeros_like(acc_ref)
    m_ref[...] = jnp.full_like(m_ref, NEG_INIT)

    def scores(j, slot, c, masked=False):
        kt = k_ref[pl.ds(pl.multiple_of(j * tile, tile), tile), :]
        s = jnp.dot(kt, qz_ref[c], preferred_element_type=F32)
        if masked:
            key = lax.broadcasted_iota(jnp.int32, s.shape, 0)
            qry = lax.broadcasted_iota(jnp.int32, s.shape, 1)
            s = jnp.where(key <= qry, s, -jnp.inf)
        s_ref[slot, c] = s
        mt_ref[slot, c] = jnp.max(s, axis=0, keepdims=True)

    def softmax_pv(j, slot, c):
        m = m_ref[c]
        m_new = jnp.maximum(m, mt_ref[slot, c])
        alpha = jnp.exp2(m - m_new)
        p = jnp.exp2(s_ref[slot, c] - m_new).astype(BF16)
        acc_ref[c] = alpha * acc_ref[c] + jnp.dot(vt_ref[j], p, preferred_element_type=F32)
        m_ref[c] = m_new

    def stage(cur, cur_slot, nxt):
        for c in range(2):
            if nxt is not None:
                scores(nxt, 1 - cur_slot, c)
            softmax_pv(cur, cur_slot, c)

    for c in range(2):
        scores(i, 0, c, masked=True)

    def pair(t, carry):
        stage(jnp.where(t == 0, i, 2 * t - 1), 0, 2 * t)
        stage(2 * t, 1, 2 * t + 1)
        return carry

    def quad(q, carry):
        return pair(2 * q + 1, pair(2 * q, carry))

    def octet(e, carry):
        return quad(2 * e + 1, quad(2 * e, carry))

    lax.fori_loop(0, i // 8, octet, 0)
    lax.fori_loop(2 * (i // 8), i // 4, quad, 0)
    lax.fori_loop(2 * (i // 4), i // 2, pair, 0)
    last = jnp.maximum(i - 1, 0)

    @pl.when(i % 2 == 0)
    def _():
        stage(last, 0, None)

    @pl.when(i % 2 == 1)
    def _():
        stage(jnp.where(i == 1, i, i - 2), 0, last)
        stage(last, 1, None)

    lp = lam_ref[...]
    lam = (jnp.exp(jnp.sum(lp[0:1] * lp[1:2], axis=-1, keepdims=True))
           - jnp.exp(jnp.sum(lp[2:3] * lp[3:4], axis=-1, keepdims=True)) + lam_init)
    a1 = acc_ref[0]
    a2 = acc_ref[1]
    o = (a1[:HEAD_WIDTH] / a1[HEAD_WIDTH:HEAD_WIDTH + 1]
         - lam * (a2[:HEAD_WIDTH] / a2[HEAD_WIDTH:HEAD_WIDTH + 1]))
    ms = jnp.mean(o * o, axis=0, keepdims=True)
    y = o * lax.rsqrt(ms + NORM_EPS) * gain_ref[...] * (1.0 - lam_init)
    o_ref[...] = y.T.astype(BF16)


def _attention(lam_params, gain, qz, k, vt, lam_init):
    n_heads, _, _, s = qz.shape
    v_rows, tile = vt.shape[-2:]
    nt = s // tile
    return pl.pallas_call(
        functools.partial(_attn_kernel, tile=tile, lam_init=lam_init),
        grid=(n_heads, nt),
        in_specs=[
            _resident(lam_params.shape),
            _resident(gain.shape),
            pl.BlockSpec((None, 2, HEAD_WIDTH, tile), lambda h, i: (h, 0, 0, i)),
            pl.BlockSpec((s, HEAD_WIDTH), lambda h, i: (0, h)),
            pl.BlockSpec((None, nt, v_rows, tile), lambda h, i: (h, 0, 0, 0)),
        ],
        out_specs=pl.BlockSpec((tile, HEAD_WIDTH), lambda h, i: (i, h)),
        out_shape=jax.ShapeDtypeStruct((s, n_heads * HEAD_WIDTH), BF16),
        scratch_shapes=[
            pltpu.VMEM((2, v_rows, tile), F32),
            pltpu.VMEM((2, 2, tile, tile), F32),
            pltpu.VMEM((2, 1, tile), F32),
            pltpu.VMEM((2, 2, 1, tile), F32),
        ],
        compiler_params=_compiler_params(2),
        name="diff_attn",
    )(lam_params, gain, qz, k, vt)


def _mix_out_ffn_kernel(x_ref, o_ref, u_ref, halo_ref, pw_ref, ps_ref, wo_ref, g_ref, wg_ref, wu_ref, wd_ref,
                        fg_ref, out_ref, *, final):
    i = pl.program_id(0)
    attn_w = o_ref.shape[1]
    y_attn = jnp.dot(o_ref[...], wo_ref[:attn_w, :], preferred_element_type=F32)
    u = u_ref[...]
    tm = u.shape[0]
    halo = jnp.where(i == 0, 0.0, halo_ref[...])
    ext = jnp.concatenate([halo, u], axis=0)
    pos = i * tm + lax.broadcasted_iota(jnp.int32, (tm, 1), 0)
    gw = pw_ref.shape[1]
    pooled = []
    for g, win in enumerate(POOL_WINDOWS):
        acc = ext[:, g * gw:(g + 1) * gw]
        span = 1
        while span < win:
            acc = acc + pltpu.roll(acc, span, 0)
            span *= 2
        count = jnp.minimum(pos + 1, win).astype(F32)
        diff = acc[POOL_HALO:] / count - u[:, g * gw:(g + 1) * gw]
        pooled.append(jnp.dot(diff.astype(BF16), pw_ref[g], preferred_element_type=F32))
    p = (jnp.concatenate(pooled, axis=1) * ps_ref[...]).astype(BF16)
    y_pool = jnp.dot(p, wo_ref[attn_w:, :], preferred_element_type=F32)
    x = x_ref[...] + (y_attn + y_pool)
    x = _swiglu_half_step(x, g_ref, wg_ref, wu_ref, wd_ref)
    out_ref[...] = _rms_norm(x, fg_ref[...]) if final else x


def _mix_out_ffn(x, o, u, pool_w, pool_scale, w_out, g, wg, wu, wd, final_g, final):
    s, d = x.shape
    tm = ROW_TILE
    halo_blocks = tm // POOL_HALO
    return pl.pallas_call(
        functools.partial(_mix_out_ffn_kernel, final=final),
        grid=(s // tm,),
        in_specs=[
            pl.BlockSpec((tm, d), lambda i: (i, 0)),
            pl.BlockSpec((tm, o.shape[1]), lambda i: (i, 0)),
            pl.BlockSpec((tm, u.shape[1]), lambda i: (i, 0)),
            pl.BlockSpec((POOL_HALO, u.shape[1]), lambda i: (jnp.maximum(i * halo_blocks - 1, 0), 0)),
            _resident(pool_w.shape),
            _resident(pool_scale.shape),
            _resident(w_out.shape),
            _resident(g.shape),
            _resident(wg.shape),
            _resident(wu.shape),
            _resident(wd.shape),
            _resident(final_g.shape),
        ],
        out_specs=pl.BlockSpec((tm, d), lambda i: (i, 0)),
        out_shape=jax.ShapeDtypeStruct((s, d), F32),
        compiler_params=_compiler_params(1),
        name="mix_out_ffn",
    )(x, o, u, u, pool_w, pool_scale, w_out, g, wg, wu, wd, final_g)


def _rope_tables(seq):
    inv_freq = ROPE_THETA ** (-jnp.arange(0, HEAD_DIM, 2, dtype=F32) / HEAD_DIM)
    ang = jnp.arange(seq, dtype=F32)[:, None] * inv_freq[None, :]
    cos = jnp.cos(ang)
    sin = jnp.sin(ang)
    return jnp.tile(cos, (1, 4)), jnp.tile(jnp.concatenate([-sin, sin], axis=1), (1, 2))


def kernel(x, ffn1_norm, ffn1_w_gate, ffn1_w_up, ffn1_w_down, mix_norm, w_in, lambda_q1, lambda_k1, lambda_q2, lambda_k2, subln_gain, pool_w, pool_scale, w_out, ffn2_norm, ffn2_w_gate, ffn2_w_up, ffn2_w_down, final_norm):
    b, s, d = x.shape
    depth = w_in.shape[0]
    attn_w = (w_in.shape[2] - w_out.shape[1]) // 2
    n_heads = attn_w // HEAD_WIDTH
    cos, sin_signed = _rope_tables(s)
    bf = lambda w: w.astype(BF16)
    assert b == 1, "the sequence kernels take one sequence; the problem fixes batch at 1"
    xb = x.reshape(s, d)
    for l in range(depth):
        xb, qz, k, vt, u = _ffn_mix_in(xb, ffn1_norm[l][None], bf(ffn1_w_gate[l]), bf(ffn1_w_up[l]),
                                       bf(ffn1_w_down[l]), mix_norm[l][None], bf(w_in[l]), cos, sin_signed, n_heads)
        lam_init = 0.8 - 0.6 * math.exp(-0.3 * l)
        lam_params = jnp.stack([lambda_q1[l], lambda_k1[l], lambda_q2[l], lambda_k2[l]]).astype(F32)
        o = _attention(lam_params, subln_gain[l].astype(F32)[:, None], qz, k, vt, lam_init)
        xb = _mix_out_ffn(xb, o, u, bf(pool_w[l]), pool_scale[l][None], bf(w_out[l]),
                          ffn2_norm[l][None], bf(ffn2_w_gate[l]), bf(ffn2_w_up[l]), bf(ffn2_w_down[l]),
                          final_norm[None], final=(l == depth - 1))
    return xb.reshape(b, s, d)
```

```python
import functools
import math

import jax
import jax.numpy as jnp
from jax import lax
from jax.experimental import pallas as pl
from jax.experimental.pallas import tpu as pltpu

F32 = jnp.float32
BF16 = jnp.bfloat16

HEAD_DIM = 64
HEAD_WIDTH = 2 * HEAD_DIM
POOL_WINDOWS = (2, 4, 8, 16)
V_ROWS = HEAD_WIDTH + 16
POOL_HALO = 16
ROPE_THETA = 10000.0
NORM_EPS = 1e-6
NEG_INIT = -1e30

VMEM_LIMIT_BYTES = 56 * 1024 * 1024
ROW_TILE = 512
ATTN_TILE = 512


def _rms_norm(x, g):
    ms = jnp.mean(x * x, axis=-1, keepdims=True)
    return x * lax.rsqrt(ms + NORM_EPS) * g


def _compiler_params(n_axes):
    return pltpu.CompilerParams(
        dimension_semantics=("arbitrary",) * n_axes,
        vmem_limit_bytes=VMEM_LIMIT_BYTES,
    )


def _resident(shape):
    return pl.BlockSpec(shape, lambda *_: (0,) * len(shape), pipeline_mode=pl.Buffered(1))


def _layer_resident(stack, layer):
    rest = stack.shape[1:]
    return pl.BlockSpec((None,) + rest, lambda *_: (layer,) + (0,) * len(rest), pipeline_mode=pl.Buffered(1))


def _swiglu_half_step(x, g_ref, wg_ref, wu_ref, wd_ref):
    h = _rms_norm(x, g_ref[...]).astype(BF16)
    gate = jnp.dot(h, wg_ref[...], preferred_element_type=F32)
    up = jnp.dot(h, wu_ref[...], preferred_element_type=F32)
    act = (gate * jax.nn.sigmoid(gate) * up).astype(BF16)
    y = jnp.dot(act, wd_ref[...], preferred_element_type=F32)
    return x + 0.5 * y


def _rope(t, cos, sin_signed, first_half):
    rot = jnp.where(first_half, pltpu.roll(t, HEAD_WIDTH - HEAD_DIM // 2, 1),
                    pltpu.roll(t, HEAD_DIM // 2, 1))
    return t * cos + rot * sin_signed


def _ffn_mix_in_kernel(x_ref, fg_ref, wg_ref, wu_ref, wd_ref, g_ref, w_ref, cos_ref, sin_ref,
                       x1_ref, qz_ref, k_ref, vt_ref, u_ref, *, n_heads):
    attn_w = n_heads * HEAD_WIDTH
    x1 = _swiglu_half_step(x_ref[...], fg_ref, wg_ref, wu_ref, wd_ref)
    x1_ref[...] = x1
    h = _rms_norm(x1, g_ref[...]).astype(BF16)
    q_all, k_all, v_all, u_all = (
        jnp.dot(h, w_ref[:, lo:hi], preferred_element_type=F32)
        for lo, hi in ((0, attn_w), (attn_w, 2 * attn_w), (2 * attn_w, 3 * attn_w), (3 * attn_w, w_ref.shape[1])))
    cos = cos_ref[...]
    sin_signed = sin_ref[...]
    tm = cos.shape[0]
    lane = lax.broadcasted_iota(jnp.int32, (tm, HEAD_WIDTH), 1)
    first_half = (lane % HEAD_DIM) < HEAD_DIM // 2
    map_row = lax.broadcasted_iota(jnp.int32, (HEAD_WIDTH, tm), 0)
    scale = HEAD_DIM ** -0.5 * math.log2(math.e)
    ones_rows = (lax.broadcasted_iota(jnp.int32, (V_ROWS - HEAD_WIDTH, tm), 0) == 0).astype(BF16)
    for hh in range(n_heads):
        c0 = hh * HEAD_WIDTH
        q = _rope(q_all[:, c0:c0 + HEAD_WIDTH], cos, sin_signed, first_half) * scale
        qt = q.T
        qz_ref[hh, 0] = jnp.where(map_row < HEAD_DIM, qt, 0.0).astype(BF16)
        qz_ref[hh, 1] = jnp.where(map_row >= HEAD_DIM, qt, 0.0).astype(BF16)
        k = _rope(k_all[:, c0:c0 + HEAD_WIDTH], cos, sin_signed, first_half)
        k_ref[:, c0:c0 + HEAD_WIDTH] = k.astype(BF16)
        v = v_all[:, c0:c0 + HEAD_WIDTH]
        vt_ref[hh, :HEAD_WIDTH] = v.T.astype(BF16)
        vt_ref[hh, HEAD_WIDTH:] = ones_rows
    u_ref[...] = u_all


def _ffn_mix_in(layer, x, ffn_g, wg, wu, wd, g, w_in, cos, sin_signed, n_heads):
    s, d = x.shape
    attn_w = n_heads * HEAD_WIDTH
    pool_w = w_in.shape[2] - 3 * attn_w
    tm = ATTN_TILE
    nt = s // tm
    return pl.pallas_call(
        functools.partial(_ffn_mix_in_kernel, n_heads=n_heads),
        grid=(nt,),
        in_specs=[
            pl.BlockSpec((tm, d), lambda i: (i, 0)),
            _resident(ffn_g.shape),
            _layer_resident(wg, layer),
            _layer_resident(wu, layer),
            _layer_resident(wd, layer),
            _resident(g.shape),
            _layer_resident(w_in, layer),
            pl.BlockSpec((tm, HEAD_WIDTH), lambda i: (i, 0)),
            pl.BlockSpec((tm, HEAD_WIDTH), lambda i: (i, 0)),
        ],
        out_specs=[
            pl.BlockSpec((tm, d), lambda i: (i, 0)),
            pl.BlockSpec((n_heads, 2, HEAD_WIDTH, tm), lambda i: (0, 0, 0, i)),
            pl.BlockSpec((tm, attn_w), lambda i: (i, 0)),
            pl.BlockSpec((n_heads, None, V_ROWS, tm), lambda i: (0, i, 0, 0)),
            pl.BlockSpec((tm, pool_w), lambda i: (i, 0)),
        ],
        out_shape=[
            jax.ShapeDtypeStruct((s, d), F32),
            jax.ShapeDtypeStruct((n_heads, 2, HEAD_WIDTH, s), BF16),
            jax.ShapeDtypeStruct((s, attn_w), BF16),
            jax.ShapeDtypeStruct((n_heads, nt, V_ROWS, tm), BF16),
            jax.ShapeDtypeStruct((s, pool_w), F32),
        ],
        compiler_params=_compiler_params(1),
        name="ffn_mix_in",
    )(x, ffn_g, wg, wu, wd, g, w_in, cos, sin_signed)


def _attn_kernel(lam_ref, gain_ref, qz_ref, k_ref, vt_ref, o_ref, acc_ref, s_ref, m_ref, mt_ref, *, tile, lam_init):
    i = pl.program_id(1)
    acc_ref[...] = jnp.zeros_like(acc_ref)
    m_ref[...] = jnp.full_like(m_ref, NEG_INIT)

    def scores(j, slot, c, masked=False):
        kt = k_ref[pl.ds(pl.multiple_of(j * tile, tile), tile), :]
        s = jnp.dot(kt, qz_ref[c], preferred_element_type=F32)
        if masked:
            key = lax.broadcasted_iota(jnp.int32, s.shape, 0)
            qry = lax.broadcasted_iota(jnp.int32, s.shape, 1)
            s = jnp.where(key <= qry, s, -jnp.inf)
        s_ref[slot, c] = s
        mt_ref[slot, c] = jnp.max(s, axis=0, keepdims=True)

    def softmax_pv(j, slot, c):
        m = m_ref[c]
        m_new = jnp.maximum(m, mt_ref[slot, c])
        alpha = jnp.exp2(m - m_new)
        p = jnp.exp2(s_ref[slot, c] - m_new).astype(BF16)
        acc_ref[c] = alpha * acc_ref[c] + jnp.dot(vt_ref[j], p, preferred_element_type=F32)
        m_ref[c] = m_new

    def stage(cur, cur_slot, nxt):
        for c in range(2):
            if nxt is not None:
                scores(nxt, 1 - cur_slot, c)
            softmax_pv(cur, cur_slot, c)

    for c in range(2):
        scores(i, 0, c, masked=True)

    def pair(t, carry):
        stage(jnp.where(t == 0, i, 2 * t - 1), 0, 2 * t)
        stage(2 * t, 1, 2 * t + 1)
        return carry

    def quad(q, carry):
        return pair(2 * q + 1, pair(2 * q, carry))

    def octet(e, carry):
        return quad(2 * e + 1, quad(2 * e, carry))

    lax.fori_loop(0, i // 8, octet, 0)
    lax.fori_loop(2 * (i // 8), i // 4, quad, 0)
    lax.fori_loop(2 * (i // 4), i // 2, pair, 0)
    last = jnp.maximum(i - 1, 0)

    @pl.when(i % 2 == 0)
    def _():
        stage(last, 0, None)

    @pl.when(i % 2 == 1)
    def _():
        stage(jnp.where(i == 1, i, i - 2), 0, last)
        stage(last, 1, None)

    lp = lam_ref[...]
    lam = (jnp.exp(jnp.sum(lp[0:1] * lp[1:2], axis=-1, keepdims=True))
           - jnp.exp(jnp.sum(lp[2:3] * lp[3:4], axis=-1, keepdims=True)) + lam_init)
    a1 = acc_ref[0]
    a2 = acc_ref[1]
    o = (a1[:HEAD_WIDTH] / a1[HEAD_WIDTH:HEAD_WIDTH + 1]
         - lam * (a2[:HEAD_WIDTH] / a2[HEAD_WIDTH:HEAD_WIDTH + 1]))
    ms = jnp.mean(o * o, axis=0, keepdims=True)
    y = o * lax.rsqrt(ms + NORM_EPS) * gain_ref[...] * (1.0 - lam_init)
    o_ref[...] = y.T.astype(BF16)


def _attention(lam_params, gain, qz, k, vt, lam_init):
    n_heads, _, _, s = qz.shape
    v_rows, tile = vt.shape[-2:]
    nt = s // tile
    return pl.pallas_call(
        functools.partial(_attn_kernel, tile=tile, lam_init=lam_init),
        grid=(n_heads, nt),
        in_specs=[
            _resident(lam_params.shape),
            _resident(gain.shape),
            pl.BlockSpec((None, 2, HEAD_WIDTH, tile), lambda h, i: (h, 0, 0, i)),
            pl.BlockSpec((s, HEAD_WIDTH), lambda h, i: (0, h)),
            pl.BlockSpec((None, nt, v_rows, tile), lambda h, i: (h, 0, 0, 0)),
        ],
        out_specs=pl.BlockSpec((tile, HEAD_WIDTH), lambda h, i: (i, h)),
        out_shape=jax.ShapeDtypeStruct((s, n_heads * HEAD_WIDTH), BF16),
        scratch_shapes=[
            pltpu.VMEM((2, v_rows, tile), F32),
            pltpu.VMEM((2, 2, tile, tile), F32),
            pltpu.VMEM((2, 1, tile), F32),
            pltpu.VMEM((2, 2, 1, tile), F32),
        ],
        compiler_params=_compiler_params(2),
        name="diff_attn",
    )(lam_params, gain, qz, k, vt)


def _mix_out_ffn_kernel(x_ref, o_ref, u_ref, halo_ref, pw_ref, ps_ref, wo_ref, g_ref, wg_ref, wu_ref, wd_ref,
                        fg_ref, out_ref, *, final):
    i = pl.program_id(0)
    attn_w = o_ref.shape[1]
    y_attn = jnp.dot(o_ref[...], wo_ref[:attn_w, :], preferred_element_type=F32)
    u = u_ref[...]
    tm = u.shape[0]
    halo = jnp.where(i == 0, 0.0, halo_ref[...])
    ext = jnp.concatenate([halo, u], axis=0)
    pos = i * tm + lax.broadcasted_iota(jnp.int32, (tm, 1), 0)
    gw = pw_ref.shape[1]
    pooled = []
    for g, win in enumerate(POOL_WINDOWS):
        acc = ext[:, g * gw:(g + 1) * gw]
        span = 1
        while span < win:
            acc = acc + pltpu.roll(acc, span, 0)
            span *= 2
        count = jnp.minimum(pos + 1, win).astype(F32)
        diff = acc[POOL_HALO:] / count - u[:, g * gw:(g + 1) * gw]
        pooled.append(jnp.dot(diff.astype(BF16), pw_ref[g], preferred_element_type=F32))
    p = (jnp.concatenate(pooled, axis=1) * ps_ref[...]).astype(BF16)
    y_pool = jnp.dot(p, wo_ref[attn_w:, :], preferred_element_type=F32)
    x = x_ref[...] + (y_attn + y_pool)
    x = _swiglu_half_step(x, g_ref, wg_ref, wu_ref, wd_ref)
    out_ref[...] = _rms_norm(x, fg_ref[...]) if final else x


def _mix_out_ffn(layer, x, o, u, pool_w, pool_scale, w_out, g, wg, wu, wd, final_g, final):
    s, d = x.shape
    tm = ROW_TILE
    halo_blocks = tm // POOL_HALO
    return pl.pallas_call(
        functools.partial(_mix_out_ffn_kernel, final=final),
        grid=(s // tm,),
        in_specs=[
            pl.BlockSpec((tm, d), lambda i: (i, 0)),
            pl.BlockSpec((tm, o.shape[1]), lambda i: (i, 0)),
            pl.BlockSpec((tm, u.shape[1]), lambda i: (i, 0)),
            pl.BlockSpec((POOL_HALO, u.shape[1]), lambda i: (jnp.maximum(i * halo_blocks - 1, 0), 0)),
            _layer_resident(pool_w, layer),
            _resident(pool_scale.shape),
            _layer_resident(w_out, layer),
            _resident(g.shape),
            _layer_resident(wg, layer),
            _layer_resident(wu, layer),
            _layer_resident(wd, layer),
            _resident(final_g.shape),
        ],
        out_specs=pl.BlockSpec((tm, d), lambda i: (i, 0)),
        out_shape=jax.ShapeDtypeStruct((s, d), F32),
        compiler_params=_compiler_params(1),
        name="mix_out_ffn",
    )(x, o, u, u, pool_w, pool_scale, w_out, g, wg, wu, wd, final_g)


def _rope_tables(seq):
    inv_freq = ROPE_THETA ** (-jnp.arange(0, HEAD_DIM, 2, dtype=F32) / HEAD_DIM)
    ang = jnp.arange(seq, dtype=F32)[:, None] * inv_freq[None, :]
    cos = jnp.cos(ang)
    sin = jnp.sin(ang)
    return jnp.tile(cos, (1, 4)), jnp.tile(jnp.concatenate([-sin, sin], axis=1), (1, 2))


def kernel(x, ffn1_norm, ffn1_w_gate, ffn1_w_up, ffn1_w_down, mix_norm, w_in, lambda_q1, lambda_k1, lambda_q2, lambda_k2, subln_gain, pool_w, pool_scale, w_out, ffn2_norm, ffn2_w_gate, ffn2_w_up, ffn2_w_down, final_norm):
    b, s, d = x.shape
    depth = w_in.shape[0]
    attn_w = (w_in.shape[2] - w_out.shape[1]) // 2
    n_heads = attn_w // HEAD_WIDTH
    cos, sin_signed = _rope_tables(s)
    bf = lambda w: w.astype(BF16)
    w1g, w1u, w1d, w2g, w2u, w2d = map(bf, (ffn1_w_gate, ffn1_w_up, ffn1_w_down, ffn2_w_gate, ffn2_w_up, ffn2_w_down))
    w_in_b, w_out_b, pool_w_b = bf(w_in), bf(w_out), bf(pool_w)
    assert b == 1, "the sequence kernels take one sequence; the problem fixes batch at 1"
    xb = x.reshape(s, d)
    for l in range(depth):
        xb, qz, k, vt, u = _ffn_mix_in(l, xb, ffn1_norm[l][None], w1g, w1u, w1d, mix_norm[l][None], w_in_b,
                                       cos, sin_signed, n_heads)
        lam_init = 0.8 - 0.6 * math.exp(-0.3 * l)
        lam_params = jnp.stack([lambda_q1[l], lambda_k1[l], lambda_q2[l], lambda_k2[l]]).astype(F32)
        o = _attention(lam_params, subln_gain[l].astype(F32)[:, None], qz, k, vt, lam_init)
        xb = _mix_out_ffn(l, xb, o, u, pool_w_b, pool_scale[l][None], w_out_b, ffn2_norm[l][None], w2g, w2u, w2d,
                          final_norm[None], final=(l == depth - 1))
    return xb.reshape(b, s, d)
```

```python
import functools
import math

import jax
import jax.numpy as jnp
from jax import lax
from jax.experimental import pallas as pl
from jax.experimental.pallas import tpu as pltpu

F32 = jnp.float32
BF16 = jnp.bfloat16

HEAD_DIM = 64
HEAD_WIDTH = 2 * HEAD_DIM
POOL_WINDOWS = (2, 4, 8, 16)
V_ROWS = HEAD_WIDTH + 16
POOL_HALO = 16
ROPE_THETA = 10000.0
NORM_EPS = 1e-6
NEG_INIT = -1e30

VMEM_LIMIT_BYTES = 56 * 1024 * 1024
ROW_TILE = 512
ATTN_TILE = 512


def _rms_norm(x, g):
    ms = jnp.mean(x * x, axis=-1, keepdims=True)
    return x * lax.rsqrt(ms + NORM_EPS) * g


def _compiler_params(n_axes):
    return pltpu.CompilerParams(
        dimension_semantics=("arbitrary",) * n_axes,
        vmem_limit_bytes=VMEM_LIMIT_BYTES,
    )


def _resident(shape):
    return pl.BlockSpec(shape, lambda *_: (0,) * len(shape), pipeline_mode=pl.Buffered(1))


def _layer_resident(stack, layer):
    rest = stack.shape[1:]
    return pl.BlockSpec((None,) + rest, lambda *_: (layer,) + (0,) * len(rest), pipeline_mode=pl.Buffered(1))


def _swiglu_half_step(x, g_ref, wg_ref, wu_ref, wd_ref):
    h = _rms_norm(x, g_ref[...]).astype(BF16)
    gate = jnp.dot(h, wg_ref[...], preferred_element_type=F32)
    up = jnp.dot(h, wu_ref[...], preferred_element_type=F32)
    act = (gate * jax.nn.sigmoid(gate) * up).astype(BF16)
    y = jnp.dot(act, wd_ref[...], preferred_element_type=F32)
    return x + 0.5 * y


def _rope(t, cos, sin_signed, first_half):
    rot = jnp.where(first_half, pltpu.roll(t, HEAD_WIDTH - HEAD_DIM // 2, 1),
                    pltpu.roll(t, HEAD_DIM // 2, 1))
    return t * cos + rot * sin_signed


def _ffn_mix_in_kernel(x_ref, fg_ref, wg_ref, wu_ref, wd_ref, g_ref, w_ref, cos_ref, sin_ref,
                       x1_ref, qz_ref, k_ref, vt_ref, u_ref, *, n_heads):
    attn_w = n_heads * HEAD_WIDTH
    x1 = _swiglu_half_step(x_ref[...], fg_ref, wg_ref, wu_ref, wd_ref)
    x1_ref[...] = x1
    h = _rms_norm(x1, g_ref[...]).astype(BF16)
    q_all, k_all, v_all, u_all = (
        jnp.dot(h, w_ref[:, lo:hi], preferred_element_type=F32)
        for lo, hi in ((0, attn_w), (attn_w, 2 * attn_w), (2 * attn_w, 3 * attn_w), (3 * attn_w, w_ref.shape[1])))
    cos = cos_ref[...]
    sin_signed = sin_ref[...]
    tm = cos.shape[0]
    lane = lax.broadcasted_iota(jnp.int32, (tm, HEAD_WIDTH), 1)
    first_half = (lane % HEAD_DIM) < HEAD_DIM // 2
    map_row = lax.broadcasted_iota(jnp.int32, (HEAD_WIDTH, tm), 0)
    scale = HEAD_DIM ** -0.5 * math.log2(math.e)
    ones_rows = (lax.broadcasted_iota(jnp.int32, (V_ROWS - HEAD_WIDTH, tm), 0) == 0).astype(BF16)
    for hh in range(n_heads):
        c0 = hh * HEAD_WIDTH
        q = _rope(q_all[:, c0:c0 + HEAD_WIDTH], cos, sin_signed, first_half) * scale
        qt = q.T
        qz_ref[hh, 0] = jnp.where(map_row < HEAD_DIM, qt, 0.0).astype(BF16)
        qz_ref[hh, 1] = jnp.where(map_row >= HEAD_DIM, qt, 0.0).astype(BF16)
        k = _rope(k_all[:, c0:c0 + HEAD_WIDTH], cos, sin_signed, first_half)
        k_ref[:, c0:c0 + HEAD_WIDTH] = k.astype(BF16)
        v = v_all[:, c0:c0 + HEAD_WIDTH]
        vt_ref[hh, :HEAD_WIDTH] = v.T.astype(BF16)
        vt_ref[hh, HEAD_WIDTH:] = ones_rows
    u_ref[...] = u_all


def _ffn_mix_in(layer, x, ffn_g, wg, wu, wd, g, w_in, cos, sin_signed, n_heads):
    s, d = x.shape
    attn_w = n_heads * HEAD_WIDTH
    pool_w = w_in.shape[2] - 3 * attn_w
    tm = ATTN_TILE
    nt = s // tm
    return pl.pallas_call(
        functools.partial(_ffn_mix_in_kernel, n_heads=n_heads),
        grid=(nt,),
        in_specs=[
            pl.BlockSpec((tm, d), lambda i: (i, 0)),
            _resident(ffn_g.shape),
            _layer_resident(wg, layer),
            _layer_resident(wu, layer),
            _layer_resident(wd, layer),
            _resident(g.shape),
            _layer_resident(w_in, layer),
            pl.BlockSpec((tm, HEAD_WIDTH), lambda i: (i, 0)),
            pl.BlockSpec((tm, HEAD_WIDTH), lambda i: (i, 0)),
        ],
        out_specs=[
            pl.BlockSpec((tm, d), lambda i: (i, 0)),
            pl.BlockSpec((n_heads, 2, HEAD_WIDTH, tm), lambda i: (0, 0, 0, i)),
            pl.BlockSpec((tm, attn_w), lambda i: (i, 0)),
            pl.BlockSpec((n_heads, None, V_ROWS, tm), lambda i: (0, i, 0, 0)),
            pl.BlockSpec((tm, pool_w), lambda i: (i, 0)),
        ],
        out_shape=[
            jax.ShapeDtypeStruct((s, d), F32),
            jax.ShapeDtypeStruct((n_heads, 2, HEAD_WIDTH, s), BF16),
            jax.ShapeDtypeStruct((s, attn_w), BF16),
            jax.ShapeDtypeStruct((n_heads, nt, V_ROWS, tm), BF16),
            jax.ShapeDtypeStruct((s, pool_w), F32),
        ],
        compiler_params=_compiler_params(1),
        name="ffn_mix_in",
    )(x, ffn_g, wg, wu, wd, g, w_in, cos, sin_signed)


def _attn_kernel(lam_ref, gain_ref, qz_ref, k_ref, vt_ref, o_ref, acc_ref, s_ref, m_ref, mt_ref, *, tile, nq, lam_init):
    n = pl.program_id(0)
    i = jnp.where(n == pl.num_programs(0) - 1, 0, n % nq)

    @pl.when(n == 0)
    def _():
        acc_ref[...] = jnp.ones_like(acc_ref)

    def scores(j, slot, c, masked=False):
        kt = k_ref[pl.ds(pl.multiple_of(j * tile, tile), tile), :]
        s = jnp.dot(kt, qz_ref[c], preferred_element_type=F32)
        if masked:
            key = lax.broadcasted_iota(jnp.int32, s.shape, 0)
            qry = lax.broadcasted_iota(jnp.int32, s.shape, 1)
            s = jnp.where(key <= qry, s, -jnp.inf)
        s_ref[slot, c] = s
        mt_ref[slot, c] = jnp.max(s, axis=0, keepdims=True)

    def softmax_pv(j, slot, c):
        m = m_ref[c]
        m_new = jnp.maximum(m, mt_ref[slot, c])
        alpha = jnp.exp2(m - m_new)
        p = jnp.exp2(s_ref[slot, c] - m_new).astype(BF16)
        acc_ref[c] = alpha * acc_ref[c] + jnp.dot(vt_ref[j], p, preferred_element_type=F32)
        m_ref[c] = m_new

    def stage(cur, cur_slot, nxt):
        for c in range(2):
            if nxt is not None:
                scores(nxt, 1 - cur_slot, c)
            softmax_pv(cur, cur_slot, c)

    for c in range(2):
        scores(i, 0, c, masked=True)

    lp = lam_ref[...]
    lam = (jnp.exp(jnp.sum(lp[0:1] * lp[1:2], axis=-1, keepdims=True))
           - jnp.exp(jnp.sum(lp[2:3] * lp[3:4], axis=-1, keepdims=True)) + lam_init)
    a1 = acc_ref[0]
    a2 = acc_ref[1]
    o = (a1[:HEAD_WIDTH] / a1[HEAD_WIDTH:HEAD_WIDTH + 1]
         - lam * (a2[:HEAD_WIDTH] / a2[HEAD_WIDTH:HEAD_WIDTH + 1]))
    ms = jnp.mean(o * o, axis=0, keepdims=True)
    y = o * lax.rsqrt(ms + NORM_EPS) * gain_ref[...] * (1.0 - lam_init)
    o_ref[...] = y.T.astype(BF16)
    acc_ref[...] = jnp.zeros_like(acc_ref)
    m_ref[...] = jnp.full_like(m_ref, NEG_INIT)

    def pair(t, carry):
        stage(jnp.where(t == 0, i, 2 * t - 1), 0, 2 * t)
        stage(2 * t, 1, 2 * t + 1)
        return carry

    def quad(q, carry):
        return pair(2 * q + 1, pair(2 * q, carry))

    def octet(e, carry):
        return quad(2 * e + 1, quad(2 * e, carry))

    lax.fori_loop(0, i // 8, octet, 0)
    lax.fori_loop(2 * (i // 8), i // 4, quad, 0)
    lax.fori_loop(2 * (i // 4), i // 2, pair, 0)
    last = jnp.maximum(i - 1, 0)

    @pl.when(i % 2 == 0)
    def _():
        stage(last, 0, None)

    @pl.when(i % 2 == 1)
    def _():
        stage(jnp.where(i == 1, i, i - 2), 0, last)
        stage(last, 1, None)


def _attention(lam_params, gain, qz, k, vt, lam_init):
    n_heads, _, _, s = qz.shape
    v_rows, tile = vt.shape[-2:]
    nt = s // tile
    total = n_heads * nt

    def cur(n):
        n = jnp.minimum(n, total - 1)
        return n // nt, n % nt

    def prev(n):
        n = jnp.maximum(n - 1, 0)
        return n // nt, n % nt

    return pl.pallas_call(
        functools.partial(_attn_kernel, tile=tile, nq=nt, lam_init=lam_init),
        grid=(total + 1,),
        in_specs=[
            _resident(lam_params.shape),
            _resident(gain.shape),
            pl.BlockSpec((None, 2, HEAD_WIDTH, tile), lambda n: (cur(n)[0], 0, 0, cur(n)[1])),
            pl.BlockSpec((s, HEAD_WIDTH), lambda n: (0, cur(n)[0])),
            pl.BlockSpec((None, nt, v_rows, tile), lambda n: (cur(n)[0], 0, 0, 0)),
        ],
        out_specs=pl.BlockSpec((tile, HEAD_WIDTH), lambda n: (prev(n)[1], prev(n)[0])),
        out_shape=jax.ShapeDtypeStruct((s, n_heads * HEAD_WIDTH), BF16),
        scratch_shapes=[
            pltpu.VMEM((2, v_rows, tile), F32),
            pltpu.VMEM((2, 2, tile, tile), F32),
            pltpu.VMEM((2, 1, tile), F32),
            pltpu.VMEM((2, 2, 1, tile), F32),
        ],
        compiler_params=_compiler_params(1),
        name="diff_attn",
    )(lam_params, gain, qz, k, vt)


def _mix_out_ffn_kernel(x_ref, o_ref, u_ref, halo_ref, pw_ref, ps_ref, wo_ref, g_ref, wg_ref, wu_ref, wd_ref,
                        fg_ref, out_ref, *, final):
    i = pl.program_id(0)
    attn_w = o_ref.shape[1]
    y_attn = jnp.dot(o_ref[...], wo_ref[:attn_w, :], preferred_element_type=F32)
    u = u_ref[...]
    tm = u.shape[0]
    halo = jnp.where(i == 0, 0.0, halo_ref[...])
    ext = jnp.concatenate([halo, u], axis=0)
    pos = i * tm + lax.broadcasted_iota(jnp.int32, (tm, 1), 0)
    gw = pw_ref.shape[1]
    pooled = []
    for g, win in enumerate(POOL_WINDOWS):
        acc = ext[:, g * gw:(g + 1) * gw]
        span = 1
        while span < win:
            acc = acc + pltpu.roll(acc, span, 0)
            span *= 2
        count = jnp.minimum(pos + 1, win).astype(F32)
        diff = acc[POOL_HALO:] / count - u[:, g * gw:(g + 1) * gw]
        pooled.append(jnp.dot(diff.astype(BF16), pw_ref[g], preferred_element_type=F32))
    p = (jnp.concatenate(pooled, axis=1) * ps_ref[...]).astype(BF16)
    y_pool = jnp.dot(p, wo_ref[attn_w:, :], preferred_element_type=F32)
    x = x_ref[...] + (y_attn + y_pool)
    x = _swiglu_half_step(x, g_ref, wg_ref, wu_ref, wd_ref)
    out_ref[...] = _rms_norm(x, fg_ref[...]) if final else x


def _mix_out_ffn(layer, x, o, u, pool_w, pool_scale, w_out, g, wg, wu, wd, final_g, final):
    s, d = x.shape
    tm = ROW_TILE
    halo_blocks = tm // POOL_HALO
    return pl.pallas_call(
        functools.partial(_mix_out_ffn_kernel, final=final),
        grid=(s // tm,),
        in_specs=[
            pl.BlockSpec((tm, d), lambda i: (i, 0)),
            pl.BlockSpec((tm, o.shape[1]), lambda i: (i, 0)),
            pl.BlockSpec((tm, u.shape[1]), lambda i: (i, 0)),
            pl.BlockSpec((POOL_HALO, u.shape[1]), lambda i: (jnp.maximum(i * halo_blocks - 1, 0), 0)),
            _layer_resident(pool_w, layer),
            _resident(pool_scale.shape),
            _layer_resident(w_out, layer),
            _resident(g.shape),
            _layer_resident(wg, layer),
            _layer_resident(wu, layer),
            _layer_resident(wd, layer),
            _resident(final_g.shape),
        ],
        out_specs=pl.BlockSpec((tm, d), lambda i: (i, 0)),
        out_shape=jax.ShapeDtypeStruct((s, d), F32),
        compiler_params=_compiler_params(1),
        name="mix_out_ffn",
    )(x, o, u, u, pool_w, pool_scale, w_out, g, wg, wu, wd, final_g)


def _rope_tables(seq):
    inv_freq = ROPE_THETA ** (-jnp.arange(0, HEAD_DIM, 2, dtype=F32) / HEAD_DIM)
    ang = jnp.arange(seq, dtype=F32)[:, None] * inv_freq[None, :]
    cos = jnp.cos(ang)
    sin = jnp.sin(ang)
    return jnp.tile(cos, (1, 4)), jnp.tile(jnp.concatenate([-sin, sin], axis=1), (1, 2))


def kernel(x, ffn1_norm, ffn1_w_gate, ffn1_w_up, ffn1_w_down, mix_norm, w_in, lambda_q1, lambda_k1, lambda_q2, lambda_k2, subln_gain, pool_w, pool_scale, w_out, ffn2_norm, ffn2_w_gate, ffn2_w_up, ffn2_w_down, final_norm):
    b, s, d = x.shape
    depth = w_in.shape[0]
    attn_w = (w_in.shape[2] - w_out.shape[1]) // 2
    n_heads = attn_w // HEAD_WIDTH
    cos, sin_signed = _rope_tables(s)
    bf = lambda w: w.astype(BF16)
    w1g, w1u, w1d, w2g, w2u, w2d = map(bf, (ffn1_w_gate, ffn1_w_up, ffn1_w_down, ffn2_w_gate, ffn2_w_up, ffn2_w_down))
    w_in_b, w_out_b, pool_w_b = bf(w_in), bf(w_out), bf(pool_w)
    assert b == 1, "the sequence kernels take one sequence; the problem fixes batch at 1"
    xb = x.reshape(s, d)
    for l in range(depth):
        xb, qz, k, vt, u = _ffn_mix_in(l, xb, ffn1_norm[l][None], w1g, w1u, w1d, mix_norm[l][None], w_in_b,
                                       cos, sin_signed, n_heads)
        lam_init = 0.8 - 0.6 * math.exp(-0.3 * l)
        lam_params = jnp.stack([lambda_q1[l], lambda_k1[l], lambda_q2[l], lambda_k2[l]]).astype(F32)
        o = _attention(lam_params, subln_gain[l].astype(F32)[:, None], qz, k, vt, lam_init)
        xb = _mix_out_ffn(l, xb, o, u, pool_w_b, pool_scale[l][None], w_out_b, ffn2_norm[l][None], w2g, w2u, w2d,
                          final_norm[None], final=(l == depth - 1))
    return xb.reshape(b, s, d)
```
